```python
import jax, jax.numpy as jnp
from jax import lax
import numpy as np

D_MODEL = 1024
BATCH = 8
SEQ = 4096
DEPTH = 2

GDN_HEADS = D_MODEL // 256
GDN_DK = 128
GDN_DV = 128
GDN_CONV = 4
GDN_CHUNK = 64
MOBA_HEADS = D_MODEL // 128
MOBA_DH = 64
MOBA_BLOCK = 256
MOBA_TOPK = 3
MOBA_QCHUNK = 32
ROPE_DIMS = MOBA_DH // 4
ROPE_THETA = 500000.0
GDN_QK_W = GDN_HEADS * GDN_DK
GDN_V_W = GDN_HEADS * GDN_DV
MOBA_W = MOBA_HEADS * MOBA_DH
D_MIX = GDN_V_W + MOBA_W
IN_SPLITS = (GDN_QK_W, GDN_QK_W, GDN_V_W, GDN_HEADS, GDN_HEADS, GDN_V_W, MOBA_W, MOBA_W, MOBA_W)
N_IN = sum(IN_SPLITS)
D_FF = 256 * ((8 * D_MODEL // 3 + 255) // 256)
FFN_CONV = 3
DEEPNORM_ALPHA = (2 * DEPTH) ** 0.25
DEEPNORM_BETA = (8 * DEPTH) ** -0.25
LN_EPS = 1e-5
NORM_EPS = 1e-6

kernel_name = 'hybrid_gdn_moba_deepnorm_convffn'


def layer_norm(x, g, b):
    xf = x.astype(jnp.float32)
    mu = xf.mean(-1, keepdims=True)
    var = jnp.square(xf - mu).mean(-1, keepdims=True)
    return ((xf - mu) * lax.rsqrt(var + LN_EPS) * g + b).astype(x.dtype)


def causal_dwconv(x, w):
    k_w = w.shape[0]
    s = x.shape[1]
    xp = jnp.pad(x, ((0, 0), (k_w - 1, 0), (0, 0)))
    return sum(xp[:, j:j + s, :] * w[j] for j in range(k_w))


def l2norm(x):
    return x * lax.rsqrt(jnp.sum(x * x, -1, keepdims=True) + NORM_EPS)


def split_heads(t, n_heads):
    b, s, _ = t.shape
    return t.reshape(b, s, n_heads, -1).transpose(0, 2, 1, 3)


def partial_rotary(x, pos):
    half = ROPE_DIMS // 2
    inv = ROPE_THETA ** (-jnp.arange(half, dtype=jnp.float32) / half)
    ang = pos.astype(jnp.float32)[:, None] * inv[None, :]
    cos = jnp.cos(ang).astype(x.dtype)
    sin = jnp.sin(ang).astype(x.dtype)
    x1, x2, rest = x[..., :half], x[..., half:ROPE_DIMS], x[..., ROPE_DIMS:]
    return jnp.concatenate([x1 * cos - x2 * sin, x2 * cos + x1 * sin, rest], -1)


def gated_delta_rule(q, k, v, g, beta):
    b_, h, s, dk = q.shape
    dv = v.shape[-1]
    c = GDN_CHUNK
    n = s // c
    q = q * dk ** -0.5
    resh = lambda t: t.reshape(b_, h, n, c, *t.shape[3:])
    q, k, v, g, beta = map(resh, (q, k, v, g, beta))
    g = jnp.cumsum(g, axis=-1)
    idx = jnp.arange(c)
    lower_incl = idx[:, None] >= idx[None, :]
    decay = jnp.exp(jnp.where(lower_incl, g[..., :, None] - g[..., None, :], -jnp.inf))
    k_beta = k * beta[..., None]
    a_strict = jnp.where(idx[:, None] > idx[None, :],
                         jnp.einsum('bhncd,bhnmd->bhncm', k_beta, k) * decay, 0.0)
    t_mat = a_strict + jnp.eye(c, dtype=q.dtype)
    rhs = jnp.concatenate([v * beta[..., None], k_beta * jnp.exp(g)[..., None]], -1)
    sol = lax.linalg.triangular_solve(t_mat, rhs, left_side=True, lower=True, unit_diagonal=True)
    u, w = sol[..., :dv], sol[..., dv:]
    attn_intra = jnp.einsum('bhncd,bhnmd->bhncm', q, k) * decay
    q_dec = q * jnp.exp(g)[..., None]
    g_last = g[..., -1]
    k_dec = k * jnp.exp(g_last[..., None] - g)[..., None]

    def step(state, xs):
        q_c, k_c, u_c, w_c, a_c, gl = xs
        v_new = u_c - jnp.einsum('bhcd,bhde->bhce', w_c, state)
        o = jnp.einsum('bhcd,bhde->bhce', q_c, state) + jnp.einsum('bhcm,bhme->bhce', a_c, v_new)
        state = state * jnp.exp(gl)[..., None, None] + jnp.einsum('bhcd,bhce->bhde', k_c, v_new)
        return state, o

    xs = tuple(jnp.moveaxis(t, 2, 0) for t in (q_dec, k_dec, u, w, attn_intra, g_last))
    s0 = jnp.zeros((b_, h, dk, dv), q.dtype)
    _, o = lax.scan(step, s0, xs)
    return jnp.moveaxis(o, 0, 2).reshape(b_, h, s, dv)


def gdn_mixer(q, k, v, a_logit, b_logit, z, conv_w, a_log, dt_bias, norm_g):
    b_, s, _ = q.shape
    qkv = jax.nn.silu(causal_dwconv(jnp.concatenate([q, k, v], -1), conv_w))
    q, k, v = jnp.split(qkv, [GDN_QK_W, 2 * GDN_QK_W], -1)
    f32 = jnp.float32
    qh = l2norm(split_heads(q, GDN_HEADS).astype(f32))
    kh = l2norm(split_heads(k, GDN_HEADS).astype(f32))
    vh = split_heads(v, GDN_HEADS).astype(f32)
    beta = jax.nn.sigmoid(b_logit.astype(f32)).transpose(0, 2, 1)
    g = (-jnp.exp(a_log.astype(f32)) * jax.nn.softplus(a_logit.astype(f32) + dt_bias.astype(f32))).transpose(0, 2, 1)
    o = gated_delta_rule(qh, kh, vh, g, beta).transpose(0, 2, 1, 3)
    o = o * lax.rsqrt(jnp.mean(o * o, -1, keepdims=True) + NORM_EPS) * norm_g.astype(f32)
    o = o * jax.nn.silu(z.reshape(b_, s, GDN_HEADS, GDN_DV).astype(f32))
    return o.reshape(b_, s, GDN_V_W).astype(q.dtype)


def moba_mixer(q, k, v, pos):
    b_, s, _ = q.shape
    h, dh, blk, qc_len = MOBA_HEADS, MOBA_DH, MOBA_BLOCK, MOBA_QCHUNK
    qh = partial_rotary(split_heads(q, h), pos)
    kh = partial_rotary(split_heads(k, h), pos)
    vh = split_heads(v, h)
    nb = -(-s // blk)
    pad = nb * blk - s
    n_sel = min(MOBA_TOPK, nb)
    kb = jnp.pad(kh, ((0, 0), (0, 0), (0, pad), (0, 0))).reshape(b_, h, nb, blk, dh)
    vb = jnp.pad(vh, ((0, 0), (0, 0), (0, pad), (0, 0))).reshape(b_, h, nb, blk, dh)
    k_mean = kb.mean(axis=3)
    nq = s // qc_len
    q_chunks = jnp.moveaxis(qh.reshape(b_, h, nq, qc_len, dh), 2, 0)
    scale = dh ** -0.5
    bi = jnp.arange(b_)[:, None, None, None]
    hi = jnp.arange(h)[None, :, None, None]
    blk_ids = jnp.arange(nb)
    sel_rank = jnp.arange(n_sel)

    def attend(args):
        q_c, ci = args
        t = ci * qc_len + jnp.arange(qc_len)
        own = (ci * qc_len) // blk
        gate = jnp.einsum('bhqd,bhnd->bhqn', q_c, k_mean).astype(jnp.float32)
        gate = jnp.where(blk_ids < own, gate, -jnp.inf)
        _, sel = lax.top_k(gate, n_sel)
        sel_valid = sel_rank < own
        k_sel = kb[bi, hi, sel]
        v_sel = vb[bi, hi, sel]
        s_sel = jnp.einsum('bhqd,bhqjkd->bhqjk', q_c, k_sel).astype(jnp.float32) * scale
        s_sel = jnp.where(sel_valid[:, None], s_sel, -jnp.inf).reshape(b_, h, qc_len, n_sel * blk)
        k_own = lax.dynamic_index_in_dim(kb, own, axis=2, keepdims=False)
        v_own = lax.dynamic_index_in_dim(vb, own, axis=2, keepdims=False)
        s_own = jnp.einsum('bhqd,bhkd->bhqk', q_c, k_own).astype(jnp.float32) * scale
        key_pos = own * blk + jnp.arange(blk)
        s_own = jnp.where(key_pos[None, :] <= t[:, None], s_own, -jnp.inf)
        p = jax.nn.softmax(jnp.concatenate([s_own, s_sel], -1), axis=-1).astype(v_sel.dtype)
        p_own = p[..., :blk]
        p_sel = p[..., blk:].reshape(b_, h, qc_len, n_sel, blk)
        return (jnp.einsum('bhqk,bhkd->bhqd', p_own, v_own)
                + jnp.einsum('bhqjk,bhqjkd->bhqd', p_sel, v_sel))

    o = lax.map(attend, (q_chunks, jnp.arange(nq)))
    o = jnp.moveaxis(o, 0, 2).reshape(b_, h, s, dh)
    return o.transpose(0, 2, 1, 3).reshape(b_, s, MOBA_W)


def hybrid_layer(x, w_in, gdn_conv_w, gdn_a_log, gdn_dt_bias, gdn_norm_g, w_out,
                 ln1_g, ln1_b, w_up, ffn_conv_w, ffn_conv_b, w_down, ln2_g, ln2_b, pos):
    proj = x @ w_in
    offs = np.cumsum(IN_SPLITS)[:-1].tolist()
    q_a, k_a, v_a, a_logit, b_logit, z, q_b, k_b, v_b = jnp.split(proj, offs, -1)
    o_a = gdn_mixer(q_a, k_a, v_a, a_logit, b_logit, z, gdn_conv_w, gdn_a_log, gdn_dt_bias, gdn_norm_g)
    o_b = moba_mixer(q_b, k_b, v_b, pos)
    mix = jnp.concatenate([o_a, o_b], -1) @ w_out
    x = layer_norm(DEEPNORM_ALPHA * x + mix, ln1_g, ln1_b)
    hid = causal_dwconv(x @ w_up, ffn_conv_w) + ffn_conv_b
    gate, val = jnp.split(hid, 2, -1)
    ffn = (jax.nn.silu(gate) * val) @ w_down
    return layer_norm(DEEPNORM_ALPHA * x + ffn, ln2_g, ln2_b)


def setup_inputs(seed: int = 0) -> dict:
    key = jax.random.key(seed)
    ks = jax.random.split(key, 16)
    f = jnp.float32
    nrm = lambda k, shape, scale: jax.random.normal(k, shape, f) * scale
    x = jax.random.normal(ks[0], (BATCH, SEQ, D_MODEL), f)
    w_in = nrm(ks[1], (DEPTH, D_MODEL, N_IN), D_MODEL ** -0.5)
    gdn_conv_w = nrm(ks[2], (DEPTH, GDN_CONV, 2 * GDN_QK_W + GDN_V_W), GDN_CONV ** -0.5)
    gdn_a_log = jnp.log(jax.random.uniform(ks[3], (DEPTH, GDN_HEADS), f, 1.0, 16.0))
    dt = jnp.exp(jax.random.uniform(ks[4], (DEPTH, GDN_HEADS), f, float(np.log(1e-3)), float(np.log(1e-1))))
    gdn_dt_bias = dt + jnp.log(-jnp.expm1(-dt))
    gdn_norm_g = 1.0 + nrm(ks[5], (DEPTH, GDN_DV), 0.02)
    w_out = nrm(ks[6], (DEPTH, D_MIX, D_MODEL), DEEPNORM_BETA * D_MIX ** -0.5)
    ln1_g = 1.0 + nrm(ks[7], (DEPTH, D_MODEL), 0.02)
    ln1_b = nrm(ks[8], (DEPTH, D_MODEL), 0.02)
    w_up = nrm(ks[9], (DEPTH, D_MODEL, 2 * D_FF), D_MODEL ** -0.5)
    ffn_conv_w = nrm(ks[10], (DEPTH, FFN_CONV, 2 * D_FF), FFN_CONV ** -0.5)
    ffn_conv_b = nrm(ks[11], (DEPTH, 2 * D_FF), 0.01)
    w_down = nrm(ks[12], (DEPTH, D_FF, D_MODEL), DEEPNORM_BETA * D_FF ** -0.5)
    ln2_g = 1.0 + nrm(ks[13], (DEPTH, D_MODEL), 0.02)
    ln2_b = nrm(ks[14], (DEPTH, D_MODEL), 0.02)
    return {'x': x, 'w_in': w_in, 'gdn_conv_w': gdn_conv_w, 'gdn_a_log': gdn_a_log,
            'gdn_dt_bias': gdn_dt_bias, 'gdn_norm_g': gdn_norm_g, 'w_out': w_out,
            'ln1_g': ln1_g, 'ln1_b': ln1_b, 'w_up': w_up, 'ffn_conv_w': ffn_conv_w,
            'ffn_conv_b': ffn_conv_b, 'w_down': w_down, 'ln2_g': ln2_g, 'ln2_b': ln2_b}


def reference(x, w_in, gdn_conv_w, gdn_a_log, gdn_dt_bias, gdn_norm_g, w_out,
              ln1_g, ln1_b, w_up, ffn_conv_w, ffn_conv_b, w_down, ln2_g, ln2_b):
    pos = jnp.arange(x.shape[1], dtype=jnp.int32)
    for l in range(DEPTH):
        x = hybrid_layer(x, w_in[l], gdn_conv_w[l], gdn_a_log[l], gdn_dt_bias[l], gdn_norm_g[l],
                         w_out[l], ln1_g[l], ln1_b[l], w_up[l], ffn_conv_w[l], ffn_conv_b[l],
                         w_down[l], ln2_g[l], ln2_b[l], pos)
    return x
```

```python
import functools

import jax
import jax.numpy as jnp
import numpy as np
from jax import lax
from jax.experimental import pallas as pl
from jax.experimental.pallas import tpu as pltpu

F32 = jnp.float32
BF16 = jnp.bfloat16
HIGHEST = lax.Precision.HIGHEST

D_MODEL = 1024
DEPTH = 2
GDN_HEADS = 4
GDN_DK = 128
GDN_DV = 128
GDN_CONV = 4
GDN_CHUNK = 64
MOBA_HEADS = 8
MOBA_DH = 64
MOBA_BLOCK = 256
MOBA_TOPK = 3
ROPE_DIMS = MOBA_DH // 4
ROPE_THETA = 500000.0
GDN_W = GDN_HEADS * GDN_DK
MOBA_W = MOBA_HEADS * MOBA_DH
D_FF = 2816
FFN_CONV = 3
DEEPNORM_ALPHA = (2 * DEPTH) ** 0.25
LN_EPS = 1e-5
NORM_EPS = 1e-6

LANES = 128
SUBLANES = 8
NEG_BIG = -1e30
VMEM_LIMIT = 56 * 1024 * 1024


def _dot(a, b, precision=None):
    return jnp.dot(a, b, preferred_element_type=F32, precision=precision)


def _dot_nt(a, b, precision=None):
    return lax.dot_general(a, b, (((1,), (1,)), ((), ())), preferred_element_type=F32,
                           precision=precision)


def _dot_tn(a, b, precision=None):
    return lax.dot_general(a, b, (((0,), (0,)), ((), ())), preferred_element_type=F32,
                           precision=precision)


def _sigmoid(x):
    return 1.0 / (1.0 + jnp.exp(-x))


def _silu(x):
    return x * _sigmoid(x)


def _layer_norm(y, g, b):
    mu = jnp.mean(y, axis=-1, keepdims=True)
    d = y - mu
    var = jnp.mean(d * d, axis=-1, keepdims=True)
    return d * lax.rsqrt(var + LN_EPS) * g + b


def _const_spec(shape):
    nd = len(shape)
    return pl.BlockSpec(shape, lambda *_: (0,) * nd, pipeline_mode=pl.Buffered(1))


def _inproj_kernel(x_ref, wm_ref, wg_ref, qkva_ref, z_ref, gates_ref, qkvb_ref):
    xb = x_ref[...].astype(BF16)
    a_w = 3 * GDN_W
    qkva_ref[...] = _dot(xb, wm_ref[:, 0:a_w])
    z_ref[...] = _dot(xb, wm_ref[:, a_w:a_w + GDN_W])
    qkvb_ref[...] = _dot(xb, wm_ref[:, a_w + GDN_W:])
    gates_ref[...] = _dot(xb, wg_ref[...])


def _inproj(x2, w_main, w_gate, tm):
    t = x2.shape[0]
    a_w = 3 * GDN_W
    b_w = 3 * MOBA_W
    row = lambda w: pl.BlockSpec((tm, w), lambda i: (i, 0))
    return pl.pallas_call(
        _inproj_kernel,
        grid=(t // tm,),
        in_specs=[row(D_MODEL), _const_spec(w_main.shape), _const_spec(w_gate.shape)],
        out_specs=[row(a_w), row(GDN_W), row(LANES), row(b_w)],
        out_shape=[jax.ShapeDtypeStruct((t, a_w), F32), jax.ShapeDtypeStruct((t, GDN_W), F32),
                   jax.ShapeDtypeStruct((t, LANES), F32), jax.ShapeDtypeStruct((t, b_w), F32)],
        compiler_params=pltpu.CompilerParams(dimension_semantics=("arbitrary",),
                                             vmem_limit_bytes=VMEM_LIMIT),
        name="inproj",
    )(x2, w_main, w_gate)


def _gdn_kernel(qkv_ref, z_ref, gates_ref, cw_ref, alog_ref, dtb_ref, ng_ref, o_ref,
                xbuf, gact, qs, ks, vs, u_s, w_s, qd_s, kd_s, at_s, egl_s, st_ref, *, tl):
    c = GDN_CHUNK
    nc = tl // c
    hist = SUBLANES

    @pl.when(pl.program_id(1) == 0)
    def _():
        xbuf[0:hist, :] = jnp.zeros((hist, 3 * GDN_W), F32)
        st_ref[...] = jnp.zeros_like(st_ref)

    xbuf[hist:hist + tl, :] = qkv_ref[...]

    gr = gates_ref[...]
    lane = lax.broadcasted_iota(jnp.int32, gr.shape, 1)
    sp_in = gr + dtb_ref[...]
    softplus = jnp.maximum(sp_in, 0.0) + jnp.log(1.0 + jnp.exp(-jnp.abs(sp_in)))
    gact[...] = jnp.where(lane < GDN_HEADS, -jnp.exp(alog_ref[...]) * softplus, _sigmoid(gr))

    for part, dst in enumerate((qs, ks, vs)):
        for h in range(GDN_HEADS):
            c0 = part * GDN_W + h * GDN_DK
            acc = None
            for j in range(GDN_CONV):
                r0 = hist - (GDN_CONV - 1) + j
                term = xbuf[r0:r0 + tl, c0:c0 + GDN_DK] * cw_ref[j:j + 1, c0:c0 + GDN_DK]
                acc = term if acc is None else acc + term
            y = _silu(acc)
            if part < 2:
                y = y * lax.rsqrt(jnp.sum(y * y, axis=-1, keepdims=True) + NORM_EPS)
            if part == 0:
                y = y * (GDN_DK ** -0.5)
            dst[:, h * GDN_DK:(h + 1) * GDN_DK] = y
    xbuf[0:hist, :] = xbuf[tl:tl + hist, :]

    ii = lax.broadcasted_iota(jnp.int32, (c, c), 0)
    jj = lax.broadcasted_iota(jnp.int32, (c, c), 1)
    ltri = (ii >= jj).astype(F32)
    utri = (ii <= jj).astype(F32)
    eye = (ii == jj).astype(F32)

    def phase1(ci, carry):
        r0 = pl.multiple_of(ci * c, c)
        rows = pl.ds(r0, c)
        gblk = gact[rows, :]
        gc_all = _dot(ltri, gblk, HIGHEST)
        gc_t = _dot_tn(gblk, utri, HIGHEST)
        for h in range(GDN_HEADS):
            cols = slice(h * GDN_DK, (h + 1) * GDN_DK)
            q = qs[rows, cols]
            k = ks[rows, cols]
            v = vs[rows, cols]
            gc = gc_all[:, h:h + 1]
            gcr = gc_t[h:h + 1, :]
            beta = gblk[:, GDN_HEADS + h:GDN_HEADS + h + 1]
            decay = jnp.exp(jnp.minimum(gc - gcr, 0.0))
            kb = k * beta
            kbf = k.astype(BF16)
            a_mat = jnp.where(ii > jj, _dot_nt(kb.astype(BF16), kbf) * decay, 0.0)
            t_mat = eye - a_mat
            pw = a_mat
            for _ in range(5):
                pw = _dot(pw, pw, HIGHEST)
                t_mat = t_mat + _dot(t_mat, pw, HIGHEST)
            egc = jnp.exp(gc)
            u = _dot(t_mat, v * beta, HIGHEST)
            w = _dot(t_mat, kb * egc, HIGHEST)
            attn = jnp.where(ii >= jj, _dot_nt(q.astype(BF16), kbf) * decay, 0.0)
            gl = gc[c - 1:c, :]
            u_s[h, rows, :] = u
            w_s[h, rows, :] = w.astype(BF16)
            qd_s[h, rows, :] = (q * egc).astype(BF16)
            kd_s[h, rows, :] = (k * jnp.exp(gl - gc)).astype(BF16)
            at_s[h, rows, :] = attn.astype(BF16)
            egl_s[h * nc + ci] = jnp.broadcast_to(jnp.exp(gl), (SUBLANES, LANES))
        return carry

    lax.fori_loop(0, nc, phase1, 0)

    def phase2(ci, carry):
        r0 = pl.multiple_of(ci * c, c)
        rows = pl.ds(r0, c)
        for h in range(GDN_HEADS):
            cols = slice(h * GDN_DV, (h + 1) * GDN_DV)
            st = st_ref[h]
            stb = st.astype(BF16)
            v_new = u_s[h, rows, :] - _dot(w_s[h, rows, :], stb)
            vnb = v_new.astype(BF16)
            o = _dot(qd_s[h, rows, :], stb) + _dot(at_s[h, rows, :], vnb)
            egl = egl_s[h * nc + ci][0:1, :]
            st_ref[h] = st * egl + _dot_tn(kd_s[h, rows, :], vnb)
            o = o * lax.rsqrt(jnp.mean(o * o, axis=-1, keepdims=True) + NORM_EPS) * ng_ref[...]
            o = o * _silu(z_ref[rows, cols])
            o_ref[rows, cols] = o.astype(o_ref.dtype)
        return carry

    lax.fori_loop(0, nc, phase2, 0)


def _gdn(qkva, z, gates, conv_w, alog_row, dtb_row, ng_row, batch, seq, tl):
    t = batch * seq
    nt = seq // tl
    nc = tl // GDN_CHUNK
    a_w = 3 * GDN_W
    row = lambda w: pl.BlockSpec((tl, w), lambda b, i: (b * nt + i, 0))
    hshape = (GDN_HEADS, tl, GDN_DK)
    return pl.pallas_call(
        functools.partial(_gdn_kernel, tl=tl),
        grid=(batch, nt),
        in_specs=[row(a_w), row(GDN_W), row(LANES), _const_spec(conv_w.shape),
                  _const_spec(alog_row.shape), _const_spec(dtb_row.shape), _const_spec(ng_row.shape)],
        out_specs=row(GDN_W),
        out_shape=jax.ShapeDtypeStruct((t, GDN_W), BF16),
        scratch_shapes=[
            pltpu.VMEM((tl + SUBLANES, a_w), F32),
            pltpu.VMEM((tl, LANES), F32),
            pltpu.VMEM((tl, GDN_W), F32),
            pltpu.VMEM((tl, GDN_W), F32),
            pltpu.VMEM((tl, GDN_W), F32),
            pltpu.VMEM(hshape, F32),
            pltpu.VMEM(hshape, BF16),
            pltpu.VMEM(hshape, BF16),
            pltpu.VMEM(hshape, BF16),
            pltpu.VMEM((GDN_HEADS, tl, GDN_CHUNK), BF16),
            pltpu.VMEM((GDN_HEADS * nc, SUBLANES, LANES), F32),
            pltpu.VMEM((GDN_HEADS, GDN_DK, GDN_DV), F32),
        ],
        compiler_params=pltpu.CompilerParams(dimension_semantics=("arbitrary", "arbitrary"),
                                             vmem_limit_bytes=VMEM_LIMIT),
        name="gdn",
    )(qkva, z, gates, conv_w, alog_row, dtb_row, ng_row)


def _rope(x, cos_t, sin_t):
    half = ROPE_DIMS // 2
    lane = lax.broadcasted_iota(jnp.int32, x.shape, 1)
    up = pltpu.roll(x, LANES - half, axis=1)
    dn = pltpu.roll(x, half, axis=1)
    partner = jnp.where((lane % MOBA_DH) < half, up, dn)
    return x * cos_t + partner * sin_t


def _moba_kernel(q_ref, k_ref, v_ref, cos_ref, sin_ref, o_ref, kr_s, vb_s, kmean_s, *, nb):
    blk = MOBA_BLOCK
    i = pl.program_id(2)

    @pl.when(i == 0)
    def _():
        kmean_s[...] = jnp.zeros_like(kmean_s)

        def prep(n, carry):
            rows = pl.ds(pl.multiple_of(n * blk, blk), blk)
            kr = _rope(k_ref[rows, :], cos_ref[rows, :], sin_ref[rows, :])
            kr_s[rows, :] = kr.astype(BF16)
            vb_s[rows, :] = v_ref[rows, :].astype(BF16)
            kmean_s[pl.ds(n, 1), :] = jnp.mean(kr, axis=0, keepdims=True)
            return carry

        lax.fori_loop(0, nb, prep, 0)

    qrows = pl.ds(pl.multiple_of(i * blk, blk), blk)
    q = _rope(q_ref[...], cos_ref[qrows, :], sin_ref[qrows, :])
    lane = lax.broadcasted_iota(jnp.int32, (blk, LANES), 1)
    head_a = lane < MOBA_DH
    q_heads = (jnp.where(head_a, q, 0.0), jnp.where(head_a, 0.0, q))

    kmean = kmean_s[...]
    bias = []
    for qh in q_heads:
        g = jnp.where(lane < i, _dot_nt(qh, kmean, HIGHEST), -jnp.inf)
        sel = jnp.zeros((blk, LANES), jnp.bool_)
        for r in range(MOBA_TOPK):
            m = jnp.max(g, axis=-1, keepdims=True)
            first = jnp.min(jnp.where(g == m, lane, LANES), axis=-1, keepdims=True)
            pick = lane == first
            sel = jnp.logical_or(sel, jnp.logical_and(pick, r < i))
            g = jnp.where(pick, -jnp.inf, g)
        bias.append(jnp.where(sel, 0.0, NEG_BIG).astype(BF16))

    scale = MOBA_DH ** -0.5
    qb = [(qh * scale).astype(BF16) for qh in q_heads]

    k_own = kr_s[qrows, :]
    v_own = vb_s[qrows, :]
    ri = lax.broadcasted_iota(jnp.int32, (blk, blk), 0)
    ci = lax.broadcasted_iota(jnp.int32, (blk, blk), 1)
    state = []
    for h in range(2):
        s = jnp.where(ci <= ri, _dot_nt(qb[h], k_own), NEG_BIG)
        m = jnp.max(s, axis=-1, keepdims=True)
        p = jnp.exp(s - m)
        l = jnp.sum(p, axis=-1, keepdims=True)
        acc = _dot(p.astype(BF16), v_own)
        state += [m, l, acc]

    q_aug = [jnp.concatenate([qb[h], bias[h]], axis=1) for h in range(2)]

    def body(n, st):
        rows = pl.ds(pl.multiple_of(n * blk, blk), blk)
        onehot = jnp.where(lane == n, 1.0, 0.0).astype(BF16)
        k_aug = jnp.concatenate([kr_s[rows, :], onehot], axis=1)
        v_n = vb_s[rows, :]
        out = []
        for h in range(2):
            m, l, acc = st[3 * h:3 * h + 3]
            s = _dot_nt(q_aug[h], k_aug)
            m_new = jnp.maximum(m, jnp.max(s, axis=-1, keepdims=True))
            alpha = jnp.exp(m - m_new)
            p = jnp.exp(s - m_new)
            l = alpha * l + jnp.sum(p, axis=-1, keepdims=True)
            acc = alpha * acc + _dot(p.astype(BF16), v_n)
            out += [m_new, l, acc]
        return tuple(out)

    state = lax.fori_loop(0, i, body, tuple(state))
    o_a = state[2] / state[1]
    o_b = state[5] / state[4]
    o_ref[...] = jnp.where(head_a, o_a, o_b).astype(o_ref.dtype)


def _moba(qkvb, cos_t, sin_t, batch, seq):
    t = batch * seq
    nb = seq // MOBA_BLOCK
    npair = MOBA_W // LANES
    return pl.pallas_call(
        functools.partial(_moba_kernel, nb=nb),
        grid=(batch, npair, nb),
        in_specs=[
            pl.BlockSpec((MOBA_BLOCK, LANES), lambda b, p, i: (b * nb + i, p)),
            pl.BlockSpec((seq, LANES), lambda b, p, i: (b, npair + p)),
            pl.BlockSpec((seq, LANES), lambda b, p, i: (b, 2 * npair + p)),
            _const_spec(cos_t.shape), _const_spec(sin_t.shape),
        ],
        out_specs=pl.BlockSpec((MOBA_BLOCK, LANES), lambda b, p, i: (b * nb + i, p)),
        out_shape=jax.ShapeDtypeStruct((t, MOBA_W), BF16),
        scratch_shapes=[
            pltpu.VMEM((seq, LANES), BF16),
            pltpu.VMEM((seq, LANES), BF16),
            pltpu.VMEM((LANES, LANES), F32),
        ],
        compiler_params=pltpu.CompilerParams(
            dimension_semantics=("arbitrary", "arbitrary", "arbitrary"),
            vmem_limit_bytes=VMEM_LIMIT),
        name="moba",
    )(qkvb, qkvb, qkvb, cos_t, sin_t)


def _outproj_kernel(oa_ref, ob_ref, x_ref, w_ref, g_ref, b_ref, o_ref):
    mix = _dot(oa_ref[...], w_ref[0:GDN_W, :]) + _dot(ob_ref[...], w_ref[GDN_W:, :])
    y = DEEPNORM_ALPHA * x_ref[...] + mix
    o_ref[...] = _layer_norm(y, g_ref[...], b_ref[...])


def _outproj(o_a, o_b, x2, w_out, g, b, tm):
    t = x2.shape[0]
    row = lambda w: pl.BlockSpec((tm, w), lambda i: (i, 0))
    return pl.pallas_call(
        _outproj_kernel,
        grid=(t // tm,),
        in_specs=[row(GDN_W), row(MOBA_W), row(D_MODEL), _const_spec(w_out.shape),
                  _const_spec(g.shape), _const_spec(b.shape)],
        out_specs=row(D_MODEL),
        out_shape=jax.ShapeDtypeStruct((t, D_MODEL), F32),
        compiler_params=pltpu.CompilerParams(dimension_semantics=("arbitrary",),
                                             vmem_limit_bytes=VMEM_LIMIT),
        name="outproj",
    )(o_a, o_b, x2, w_out, g, b)


def _ffn_kernel(x_ref, wup_ref, cw_ref, cb_ref, wdn_ref, g_ref, b_ref, o_ref,
                buf_g, buf_v, carry_g, carry_v, acc_ref, *, tm, seq, tf):
    nf = D_FF // tf
    hist = SUBLANES
    seq_start = (pl.program_id(0) * tm) % seq == 0
    x = x_ref[...]
    xb = x.astype(BF16)
    for f in range(nf):
        halves = []
        for buf, carry, off in ((buf_g, carry_g, 0), (buf_v, carry_v, D_FF)):
            cols = slice(off + f * tf, off + (f + 1) * tf)
            up = _dot(xb, wup_ref[:, cols])
            buf[0:hist, :] = jnp.where(seq_start, 0.0, carry[f])
            buf[hist:hist + tm, :] = up
            carry[f] = up[tm - hist:tm, :]
            y = up * cw_ref[FFN_CONV - 1:FFN_CONV, cols]
            for j in range(FFN_CONV - 1):
                r0 = hist - (FFN_CONV - 1) + j
                y = y + buf[r0:r0 + tm, :] * cw_ref[j:j + 1, cols]
            halves.append(y + cb_ref[:, cols])
        hid = (_silu(halves[0]) * halves[1]).astype(BF16)
        part = _dot(hid, wdn_ref[f * tf:(f + 1) * tf, :])
        if f == 0:
            acc_ref[...] = part
        else:
            acc_ref[...] += part
    y = DEEPNORM_ALPHA * x + acc_ref[...]
    o_ref[...] = _layer_norm(y, g_ref[...], b_ref[...])


def _ffn(x2, w_up, conv_w, conv_b, w_down, g, b, seq, tm, tf):
    t = x2.shape[0]
    nf = D_FF // tf
    row = pl.BlockSpec((tm, D_MODEL), lambda i: (i, 0))
    return pl.pallas_call(
        functools.partial(_ffn_kernel, tm=tm, seq=seq, tf=tf),
        grid=(t // tm,),
        in_specs=[row, _const_spec(w_up.shape), _const_spec(conv_w.shape), _const_spec(conv_b.shape),
                  _const_spec(w_down.shape), _const_spec(g.shape), _const_spec(b.shape)],
        out_specs=row,
        out_shape=jax.ShapeDtypeStruct((t, D_MODEL), F32),
        scratch_shapes=[
            pltpu.VMEM((tm + SUBLANES, tf), F32), pltpu.VMEM((tm + SUBLANES, tf), F32),
            pltpu.VMEM((nf, SUBLANES, tf), F32), pltpu.VMEM((nf, SUBLANES, tf), F32),
            pltpu.VMEM((tm, D_MODEL), F32),
        ],
        compiler_params=pltpu.CompilerParams(dimension_semantics=("arbitrary",),
                                             vmem_limit_bytes=VMEM_LIMIT),
        name="ffn",
    )(x2, w_up, conv_w, conv_b, w_down, g, b)


def _rope_tables(seq):
    half = ROPE_DIMS // 2
    inv = ROPE_THETA ** (-jnp.arange(half, dtype=F32) / half)
    ang = jnp.arange(seq, dtype=jnp.int32).astype(F32)[:, None] * inv[None, :]
    cos, sin = jnp.cos(ang), jnp.sin(ang)
    rest = MOBA_DH - ROPE_DIMS
    cos_h = jnp.concatenate([cos, cos, jnp.ones((seq, rest), F32)], -1)
    sin_h = jnp.concatenate([-sin, sin, jnp.zeros((seq, rest), F32)], -1)
    reps = LANES // MOBA_DH
    return jnp.tile(cos_h, (1, reps)), jnp.tile(sin_h, (1, reps))


def _pad_lanes(v):
    return jnp.pad(v.astype(F32), (0, LANES - v.shape[0]))[None, :]


def kernel(x, w_in, gdn_conv_w, gdn_a_log, gdn_dt_bias, gdn_norm_g, w_out, ln1_g, ln1_b,
           w_up, ffn_conv_w, ffn_conv_b, w_down, ln2_g, ln2_b):
    batch, seq, d = x.shape
    t = batch * seq
    tm = min(512, seq)
    tl = min(256, seq)
    tf = 256
    cos_t, sin_t = _rope_tables(seq)
    g_off = 3 * GDN_W
    z_off = g_off + 2 * GDN_HEADS
    x2 = x.reshape(t, d)
    for l in range(DEPTH):
        w = w_in[l]
        w_main = jnp.concatenate([w[:, :g_off], w[:, z_off:]], axis=1).astype(BF16)
        w_gate = jnp.pad(w[:, g_off:z_off], ((0, 0), (0, LANES - 2 * GDN_HEADS))).astype(BF16)
        qkva, z, gates, qkvb = _inproj(x2, w_main, w_gate, tm)
        o_a = _gdn(qkva, z, gates, gdn_conv_w[l], _pad_lanes(gdn_a_log[l]), _pad_lanes(gdn_dt_bias[l]),
                   gdn_norm_g[l][None, :], batch, seq, tl)
        o_b = _moba(qkvb, cos_t, sin_t, batch, seq)
        x1 = _outproj(o_a, o_b, x2, w_out[l].astype(BF16), ln1_g[l][None, :], ln1_b[l][None, :], tm)
        x2 = _ffn(x1, w_up[l].astype(BF16), ffn_conv_w[l], ffn_conv_b[l][None, :],
                  w_down[l].astype(BF16), ln2_g[l][None, :], ln2_b[l][None, :], seq, tm, tf)
    return x2.reshape(batch, seq, d)
```

```python
import functools

import jax
import jax.numpy as jnp
import numpy as np
from jax import lax
from jax.experimental import pallas as pl
from jax.experimental.pallas import tpu as pltpu

F32 = jnp.float32
BF16 = jnp.bfloat16
HIGHEST = lax.Precision.HIGHEST

D_MODEL = 1024
DEPTH = 2
GDN_HEADS = 4
GDN_DK = 128
GDN_DV = 128
GDN_CONV = 4
GDN_CHUNK = 64
MOBA_HEADS = 8
MOBA_DH = 64
MOBA_BLOCK = 256
MOBA_TOPK = 3
ROPE_DIMS = MOBA_DH // 4
ROPE_THETA = 500000.0
GDN_W = GDN_HEADS * GDN_DK
MOBA_W = MOBA_HEADS * MOBA_DH
D_FF = 2816
FFN_CONV = 3
DEEPNORM_ALPHA = (2 * DEPTH) ** 0.25
LN_EPS = 1e-5
NORM_EPS = 1e-6

LANES = 128
SUBLANES = 8
NEG_BIG = -1e30
VMEM_LIMIT = 56 * 1024 * 1024


def _dot(a, b, precision=None):
    return jnp.dot(a, b, preferred_element_type=F32, precision=precision)


def _dot_nt(a, b, precision=None):
    return lax.dot_general(a, b, (((1,), (1,)), ((), ())), preferred_element_type=F32,
                           precision=precision)


def _dot_tn(a, b, precision=None):
    return lax.dot_general(a, b, (((0,), (0,)), ((), ())), preferred_element_type=F32,
                           precision=precision)


def _split2(x):
    hi = x.astype(BF16)
    return hi, (x - hi.astype(F32)).astype(BF16)


def _split3(x):
    hi = x.astype(BF16)
    r1 = x - hi.astype(F32)
    mid = r1.astype(BF16)
    return hi, mid, (r1 - mid.astype(F32)).astype(BF16)


def _dot3(a, b):
    return _dot(a[0], b[0]) + (_dot(a[0], b[1]) + _dot(a[1], b[0]))


def _sigmoid(x):
    return 1.0 / (1.0 + jnp.exp(-x))


def _silu(x):
    return x * _sigmoid(x)


def _layer_norm(y, g, b):
    mu = jnp.mean(y, axis=-1, keepdims=True)
    d = y - mu
    var = jnp.mean(d * d, axis=-1, keepdims=True)
    return d * lax.rsqrt(var + LN_EPS) * g + b


def _const_spec(shape):
    nd = len(shape)
    return pl.BlockSpec(shape, lambda *_: (0,) * nd, pipeline_mode=pl.Buffered(1))


def _inproj_kernel(x_ref, wm_ref, wg_ref, qkva_ref, z_ref, gates_ref, qkvb_ref):
    xb = x_ref[...].astype(BF16)
    a_w = 3 * GDN_W
    qkva_ref[...] = _dot(xb, wm_ref[:, 0:a_w])
    z_ref[...] = _dot(xb, wm_ref[:, a_w:a_w + GDN_W])
    qkvb_ref[...] = _dot(xb, wm_ref[:, a_w + GDN_W:])
    gates_ref[...] = _dot(xb, wg_ref[...])


def _inproj(x2, w_main, w_gate, tm):
    t = x2.shape[0]
    a_w = 3 * GDN_W
    b_w = 3 * MOBA_W
    row = lambda w: pl.BlockSpec((tm, w), lambda i: (i, 0))
    return pl.pallas_call(
        _inproj_kernel,
        grid=(t // tm,),
        in_specs=[row(D_MODEL), _const_spec(w_main.shape), _const_spec(w_gate.shape)],
        out_specs=[row(a_w), row(GDN_W), row(LANES), row(b_w)],
        out_shape=[jax.ShapeDtypeStruct((t, a_w), F32), jax.ShapeDtypeStruct((t, GDN_W), F32),
                   jax.ShapeDtypeStruct((t, LANES), F32), jax.ShapeDtypeStruct((t, b_w), F32)],
        compiler_params=pltpu.CompilerParams(dimension_semantics=("arbitrary",),
                                             vmem_limit_bytes=VMEM_LIMIT),
        name="inproj",
    )(x2, w_main, w_gate)


def _gdn_kernel(qkv_ref, z_ref, gates_ref, cw_ref, alog_ref, dtb_ref, ng_ref, o_ref,
                xbuf, gact, qs, ks, vs, u_s, w_s, qd_s, kd_s, at_s, egl_s, st_ref, o_s, *, tl, cpi):
    c = GDN_CHUNK
    nc = tl // c
    hist = SUBLANES

    @pl.when(pl.program_id(1) == 0)
    def _():
        xbuf[0:hist, :] = jnp.zeros((hist, 3 * GDN_W), F32)
        st_ref[...] = jnp.zeros_like(st_ref)

    xbuf[hist:hist + tl, :] = qkv_ref[...]

    gr = gates_ref[...]
    lane = lax.broadcasted_iota(jnp.int32, gr.shape, 1)
    sp_in = gr + dtb_ref[...]
    softplus = jnp.maximum(sp_in, 0.0) + jnp.log(1.0 + jnp.exp(-jnp.abs(sp_in)))
    gact[...] = jnp.where(lane < GDN_HEADS, -jnp.exp(alog_ref[...]) * softplus, _sigmoid(gr))

    for part, dst in enumerate((qs, ks, vs)):
        for h in range(GDN_HEADS):
            c0 = part * GDN_W + h * GDN_DK
            acc = None
            for j in range(GDN_CONV):
                r0 = hist - (GDN_CONV - 1) + j
                term = xbuf[r0:r0 + tl, c0:c0 + GDN_DK] * cw_ref[j:j + 1, c0:c0 + GDN_DK]
                acc = term if acc is None else acc + term
            y = _silu(acc)
            if part < 2:
                y = y * lax.rsqrt(jnp.sum(y * y, axis=-1, keepdims=True) + NORM_EPS)
            if part == 0:
                y = y * (GDN_DK ** -0.5)
            dst[:, h * GDN_DK:(h + 1) * GDN_DK] = y
    xbuf[0:hist, :] = xbuf[tl:tl + hist, :]

    ii = lax.broadcasted_iota(jnp.int32, (c, LANES), 0)
    lane = lax.broadcasted_iota(jnp.int32, (c, LANES), 1)
    jj = lane % c
    left = lane < c
    mask_l = jnp.where(left, 1.0, 0.0).astype(BF16)
    mask_r = jnp.where(left, 0.0, 1.0).astype(BF16)
    eye_p = jnp.where(ii == jj, 1.0, 0.0)
    i3 = lax.broadcasted_iota(jnp.int32, (c, 3 * c), 0)
    t3 = lax.broadcasted_iota(jnp.int32, (c, 3 * c), 1) % c
    ltri3 = jnp.where(i3 >= t3, 1.0, 0.0).astype(BF16)

    def blockdiag(p):
        return jnp.concatenate([p * mask_l, p * mask_r], axis=0)

    def blockdiag_wide(x):
        n = x.shape[1] // 2
        z = jnp.zeros((c, n), BF16)
        return jnp.concatenate([jnp.concatenate([x[:, :n], z], axis=1),
                                jnp.concatenate([z, x[:, n:]], axis=1)], axis=0)

    def dot3(a, b_hi, b_lo):
        lhs = jnp.concatenate([a[0], a[1], a[0]], axis=1)
        return _dot(lhs, jnp.concatenate([b_hi, b_hi, b_lo], axis=0))

    def pair_gates(rows, p):
        gblk = gact[rows, :]
        gc_all = _dot(ltri3, jnp.concatenate(_split3(gblk), axis=0))
        bc = lambda a, l: jnp.broadcast_to(a[:, l:l + 1], (c, LANES))
        gc_cat = jnp.concatenate([bc(gc_all, 2 * p), bc(gc_all, 2 * p + 1)], axis=1)
        b_cat = jnp.concatenate([bc(gblk, GDN_HEADS + 2 * p), bc(gblk, GDN_HEADS + 2 * p + 1)], axis=1)
        return gc_cat, b_cat

    def phase1(it, carry):
        chains = [(it * cpi + cc, p) for cc in range(cpi) for p in range(GDN_HEADS // 2)]
        t_mats, pws, rhs = [], [], []
        for ci, p in chains:
            rows = pl.ds(pl.multiple_of(ci * c, c), c)
            cols = slice(p * 2 * GDN_DK, (p + 1) * 2 * GDN_DK)
            gc_cat, b_cat = pair_gates(rows, p)
            gc_p = jnp.where(left, gc_cat[:, :LANES], gc_cat[:, LANES:])
            gcr_p = jnp.sum(jnp.where(ii == jj, gc_p, 0.0), axis=0, keepdims=True)
            decay = jnp.exp(jnp.minimum(gc_p - gcr_p, 0.0))
            q = qs[rows, cols]
            k = ks[rows, cols]
            kb = k * b_cat
            kq = _dot_nt(jnp.concatenate([kb.astype(BF16), q.astype(BF16)], axis=0),
                         blockdiag_wide(k.astype(BF16)))
            a_mat = jnp.where(ii > jj, kq[:c] * decay, 0.0)
            at_s[p, rows, :] = jnp.where(ii >= jj, kq[c:] * decay, 0.0).astype(BF16)
            gl = gc_cat[c - 1:c, :]
            egc = jnp.exp(gc_cat)
            qd_s[p, rows, :] = (q * egc).astype(BF16)
            kd_s[p, rows, :] = (k * jnp.exp(gl - gc_cat)).astype(BF16)
            egl_s[p * nc + ci] = jnp.broadcast_to(jnp.exp(gl), (SUBLANES, 2 * LANES))
            vb = vs[rows, cols] * b_cat
            ke = kb * egc
            x = _split2(jnp.concatenate([vb[:, :GDN_DV], ke[:, :GDN_DK], vb[:, GDN_DV:], ke[:, GDN_DK:]],
                                        axis=1))
            rhs.append((blockdiag_wide(x[0]), blockdiag_wide(x[1])))
            t_mats.append(eye_p - a_mat)
            pws.append(_split2(a_mat))
        for _ in range(5):
            pws = [_split2(dot3(pw, blockdiag(pw[0]), blockdiag(pw[1]))) for pw in pws]
            t_mats = [t + dot3(_split2(t), blockdiag(pw[0]), blockdiag(pw[1]))
                      for t, pw in zip(t_mats, pws)]
        for (ci, p), t_mat, x in zip(chains, t_mats, rhs):
            rows = pl.ds(pl.multiple_of(ci * c, c), c)
            sol = dot3(_split2(t_mat), x[0], x[1])
            u_s[p, rows, :] = jnp.concatenate([sol[:, 0:128], sol[:, 256:384]], axis=1)
            w_s[p, rows, :] = jnp.concatenate([sol[:, 128:256], sol[:, 384:512]], axis=1).astype(BF16)
        return carry

    lax.fori_loop(0, nc // cpi, phase1, 0)

    zs = jnp.zeros((GDN_DK, GDN_DV), BF16)

    def phase2(ci, carry):
        rows = pl.ds(pl.multiple_of(ci * c, c), c)
        pairs = range(GDN_HEADS // 2)
        sts = [(st_ref[2 * p], st_ref[2 * p + 1]) for p in pairs]
        rs = []
        for p in pairs:
            bds = jnp.concatenate([jnp.concatenate([sts[p][0].astype(BF16), zs], axis=1),
                                   jnp.concatenate([zs, sts[p][1].astype(BF16)], axis=1)], axis=0)
            rs.append(_dot(jnp.concatenate([w_s[p, rows, :], qd_s[p, rows, :]], axis=0), bds))
        vnbs = [(u_s[p, rows, :] - rs[p][:c]).astype(BF16) for p in pairs]
        upds = [_dot_tn(kd_s[p, rows, :], vnbs[p]) for p in pairs]
        for p in pairs:
            o_s[rows, p * 2 * GDN_DV:(p + 1) * 2 * GDN_DV] = (
                rs[p][c:] + _dot(at_s[p, rows, :], blockdiag_wide(vnbs[p])))
        for p in pairs:
            egl = egl_s[p * nc + ci][0:1, :]
            st_ref[2 * p] = sts[p][0] * egl[:, :GDN_DV] + upds[p][:GDN_DK, :GDN_DV]
            st_ref[2 * p + 1] = sts[p][1] * egl[:, GDN_DV:] + upds[p][GDN_DK:, GDN_DV:]
        return carry

    lax.fori_loop(0, nc, phase2, 0)

    for h in range(GDN_HEADS):
        cols = slice(h * GDN_DV, (h + 1) * GDN_DV)
        o = o_s[:, cols]
        o = o * lax.rsqrt(jnp.mean(o * o, axis=-1, keepdims=True) + NORM_EPS) * ng_ref[...]
        o_ref[:, cols] = (o * _silu(z_ref[:, cols])).astype(o_ref.dtype)


def _gdn(qkva, z, gates, conv_w, alog_row, dtb_row, ng_row, batch, seq, tl):
    t = batch * seq
    nt = seq // tl
    nc = tl // GDN_CHUNK
    a_w = 3 * GDN_W
    row = lambda w: pl.BlockSpec((tl, w), lambda b, i: (b * nt + i, 0))
    npair = GDN_HEADS // 2
    pshape = (npair, tl, 2 * GDN_DK)
    return pl.pallas_call(
        functools.partial(_gdn_kernel, tl=tl, cpi=min(4, nc)),
        grid=(batch, nt),
        in_specs=[row(a_w), row(GDN_W), row(LANES), _const_spec(conv_w.shape),
                  _const_spec(alog_row.shape), _const_spec(dtb_row.shape), _const_spec(ng_row.shape)],
        out_specs=row(GDN_W),
        out_shape=jax.ShapeDtypeStruct((t, GDN_W), BF16),
        scratch_shapes=[
            pltpu.VMEM((tl + SUBLANES, a_w), F32),
            pltpu.VMEM((tl, LANES), F32),
            pltpu.VMEM((tl, GDN_W), F32),
            pltpu.VMEM((tl, GDN_W), F32),
            pltpu.VMEM((tl, GDN_W), F32),
            pltpu.VMEM(pshape, F32),
            pltpu.VMEM(pshape, BF16),
            pltpu.VMEM(pshape, BF16),
            pltpu.VMEM(pshape, BF16),
            pltpu.VMEM((npair, tl, 2 * GDN_CHUNK), BF16),
            pltpu.VMEM((npair * nc, SUBLANES, 2 * LANES), F32),
            pltpu.VMEM((GDN_HEADS, GDN_DK, GDN_DV), F32),
            pltpu.VMEM((tl, GDN_W), F32),
        ],
        compiler_params=pltpu.CompilerParams(dimension_semantics=("arbitrary", "arbitrary"),
                                             vmem_limit_bytes=VMEM_LIMIT),
        name="gdn",
    )(qkva, z, gates, conv_w, alog_row, dtb_row, ng_row)


def _rope(x, cos_t, sin_t):
    half = ROPE_DIMS // 2
    lane = lax.broadcasted_iota(jnp.int32, x.shape, 1)
    up = pltpu.roll(x, LANES - half, axis=1)
    dn = pltpu.roll(x, half, axis=1)
    partner = jnp.where((lane % MOBA_DH) < half, up, dn)
    return x * cos_t + partner * sin_t


def _moba_kernel(q_ref, k_ref, v_ref, cos_ref, sin_ref, o_ref, kr_s, vb_s, kmean_s,
                 qa_s, s_all, m_run, l_acc, acc_s, *, nb):
    blk = MOBA_BLOCK
    lane = lax.broadcasted_iota(jnp.int32, (blk, LANES), 1)
    head_a = lane < MOBA_DH
    scale = MOBA_DH ** -0.5

    kmean_s[...] = jnp.zeros_like(kmean_s)

    def prep(n, carry):
        rows = pl.ds(pl.multiple_of(n * blk, blk), blk)
        kr = _rope(k_ref[rows, :], cos_ref[rows, :], sin_ref[rows, :])
        kr_s[rows, :] = kr.astype(BF16)
        vb_s[rows, :] = v_ref[rows, :].astype(BF16)
        kmean_s[pl.ds(n, 1), :] = jnp.mean(kr, axis=0, keepdims=True)
        return carry

    lax.fori_loop(0, nb, prep, 0)

    nbp = -(-nb // SUBLANES) * SUBLANES
    kmean = kmean_s[0:nbp, :]
    brow = lax.broadcasted_iota(jnp.int32, (nbp, blk), 0)

    def gating(it, carry):
        items = []
        for d in range(2):
            i = 2 * it + d
            qrows = pl.ds(pl.multiple_of(i * blk, blk), blk)
            q = _rope(q_ref[qrows, :], cos_ref[qrows, :], sin_ref[qrows, :])
            for h, qh in enumerate((jnp.where(head_a, q, 0.0), jnp.where(head_a, 0.0, q))):
                qa_s[h, qrows, 0:LANES] = (qh * scale).astype(BF16)
                items.append((i, h, qrows, qh))
        gs = [jnp.where(brow < i, _dot_nt(kmean, qh, HIGHEST), -jnp.inf) for i, _, _, qh in items]
        sels = [jnp.zeros((nbp, blk), jnp.bool_) for _ in items]
        for r in range(MOBA_TOPK):
            for n, (i, _, _, _) in enumerate(items):
                g = gs[n]
                m = jnp.max(g, axis=0, keepdims=True)
                first = jnp.min(jnp.where(g == m, brow, nbp), axis=0, keepdims=True)
                pick = brow == first
                sels[n] = jnp.logical_or(sels[n], jnp.logical_and(pick, r < i))
                gs[n] = jnp.where(pick, -jnp.inf, g)
        for (i, h, qrows, _), sel in zip(items, sels):
            bias_t = jnp.concatenate([jnp.where(sel, 0.0, NEG_BIG),
                                      jnp.zeros((LANES - nbp, blk), F32)], axis=0)
            qa_s[h, qrows, LANES:] = bias_t.T.astype(BF16)
        return carry

    lax.fori_loop(0, nb // 2, gating, 0)

    ri = lax.broadcasted_iota(jnp.int32, (blk, blk), 0)
    ci = lax.broadcasted_iota(jnp.int32, (blk, blk), 1)
    lane2 = lax.broadcasted_iota(jnp.int32, (2 * blk, LANES), 1)
    half2 = lax.broadcasted_iota(jnp.int32, (2 * blk, LANES), 0) // blk
    neg = jnp.full((blk, blk), NEG_BIG, F32)

    def row_max4(s):
        return jnp.maximum(jnp.maximum(s[:, 0:LANES], s[:, LANES:2 * LANES]),
                           jnp.maximum(s[:, 2 * LANES:3 * LANES], s[:, 3 * LANES:]))

    def qblock(i, carry):
        qrows = pl.ds(pl.multiple_of(i * blk, blk), blk)

        for h in range(2):
            m_run[h] = jnp.full((blk, LANES), NEG_BIG, F32)

        def pass1(j, c):
            rows = pl.ds(pl.multiple_of(j * 2 * blk, 2 * blk), 2 * blk)
            onehot = jnp.where(lane2 == 2 * j + half2, 1.0, 0.0).astype(BF16)
            k_aug = jnp.concatenate([kr_s[rows, :], onehot], axis=1)
            for h in range(2):
                s = _dot_nt(qa_s[h, qrows, :], k_aug)
                s_all[h, 2 * j] = s[:, :blk]
                s_all[h, 2 * j + 1] = s[:, blk:]
                m_run[h] = jnp.maximum(m_run[h], row_max4(s))
            return c

        lax.fori_loop(0, (i + 1) // 2, pass1, 0)

        k_own = kr_s[qrows, :]
        for h in range(2):
            s = jnp.where(ci <= ri, _dot_nt(qa_s[h, qrows, 0:LANES], k_own), NEG_BIG)
            s_all[h, i] = s
            m_run[h] = jnp.maximum(m_run[h], jnp.maximum(s[:, :LANES], s[:, LANES:]))

        @pl.when(i % 2 == 0)
        def _():
            for h in range(2):
                s_all[h, i + 1] = neg

        for h in range(2):
            m_run[h] = jnp.broadcast_to(jnp.max(m_run[h], axis=-1, keepdims=True), (blk, LANES))
            l_acc[h] = jnp.zeros((blk, LANES), F32)
            acc_s[h] = jnp.zeros((blk, LANES), F32)

        def pass2(j, c):
            rows = pl.ds(pl.multiple_of(j * 2 * blk, 2 * blk), 2 * blk)
            v2 = vb_s[rows, :]
            for h in range(2):
                mb = m_run[h]
                ps = [jnp.exp(s_all[h, 2 * j + b, :, t * LANES:(t + 1) * LANES] - mb)
                      for b in range(2) for t in range(2)]
                l_acc[h] += (ps[0] + ps[1]) + (ps[2] + ps[3])
                acc_s[h] += _dot(jnp.concatenate(ps, axis=1).astype(BF16), v2)
            return c

        lax.fori_loop(0, i // 2 + 1, pass2, 0)
        o_a = acc_s[0] / jnp.sum(l_acc[0], axis=-1, keepdims=True)
        o_b = acc_s[1] / jnp.sum(l_acc[1], axis=-1, keepdims=True)
        o_ref[qrows, :] = jnp.where(head_a, o_a, o_b).astype(o_ref.dtype)
        return carry

    lax.fori_loop(0, nb, qblock, 0)


def _moba(qkvb, cos_t, sin_t, batch, seq):
    t = batch * seq
    nb = seq // MOBA_BLOCK
    assert nb % 2 == 0, "key blocks are processed in pairs"
    npair = MOBA_W // LANES
    col = lambda off: pl.BlockSpec((seq, LANES), lambda b, p: (b, off + p))
    return pl.pallas_call(
        functools.partial(_moba_kernel, nb=nb),
        grid=(batch, npair),
        in_specs=[col(0), col(npair), col(2 * npair), _const_spec(cos_t.shape), _const_spec(sin_t.shape)],
        out_specs=col(0),
        out_shape=jax.ShapeDtypeStruct((t, MOBA_W), BF16),
        scratch_shapes=[
            pltpu.VMEM((seq, LANES), BF16),
            pltpu.VMEM((seq, LANES), BF16),
            pltpu.VMEM((LANES, LANES), F32),
            pltpu.VMEM((2, seq, 2 * LANES), BF16),
            pltpu.VMEM((2, nb, MOBA_BLOCK, MOBA_BLOCK), F32),
            pltpu.VMEM((2, MOBA_BLOCK, LANES), F32),
            pltpu.VMEM((2, MOBA_BLOCK, LANES), F32),
            pltpu.VMEM((2, MOBA_BLOCK, LANES), F32),
        ],
        compiler_params=pltpu.CompilerParams(dimension_semantics=("arbitrary", "arbitrary"),
                                             vmem_limit_bytes=VMEM_LIMIT),
        name="moba",
    )(qkvb, qkvb, qkvb, cos_t, sin_t)


def _outproj_kernel(oa_ref, ob_ref, x_ref, w_ref, g_ref, b_ref, o_ref):
    mix = _dot(oa_ref[...], w_ref[0:GDN_W, :]) + _dot(ob_ref[...], w_ref[GDN_W:, :])
    y = DEEPNORM_ALPHA * x_ref[...] + mix
    o_ref[...] = _layer_norm(y, g_ref[...], b_ref[...])


def _outproj(o_a, o_b, x2, w_out, g, b, tm):
    t = x2.shape[0]
    row = lambda w: pl.BlockSpec((tm, w), lambda i: (i, 0))
    return pl.pallas_call(
        _outproj_kernel,
        grid=(t // tm,),
        in_specs=[row(GDN_W), row(MOBA_W), row(D_MODEL), _const_spec(w_out.shape),
                  _const_spec(g.shape), _const_spec(b.shape)],
        out_specs=row(D_MODEL),
        out_shape=jax.ShapeDtypeStruct((t, D_MODEL), F32),
        compiler_params=pltpu.CompilerParams(dimension_semantics=("arbitrary",),
                                             vmem_limit_bytes=VMEM_LIMIT),
        name="outproj",
    )(o_a, o_b, x2, w_out, g, b)


def _ffn_kernel(x_ref, wup_ref, cw_ref, cb_ref, wdn_ref, g_ref, b_ref, o_ref,
                buf_g, buf_v, carry_g, carry_v, acc_ref, *, tm, seq, tf):
    nf = D_FF // tf
    hist = SUBLANES
    seq_start = (pl.program_id(0) * tm) % seq == 0
    x = x_ref[...]
    xb = x.astype(BF16)
    for f in range(nf):
        halves = []
        for buf, carry, off in ((buf_g, carry_g, 0), (buf_v, carry_v, D_FF)):
            cols = slice(off + f * tf, off + (f + 1) * tf)
            up = _dot(xb, wup_ref[:, cols])
            buf[0:hist, :] = jnp.where(seq_start, 0.0, carry[f])
            buf[hist:hist + tm, :] = up
            carry[f] = up[tm - hist:tm, :]
            y = up * cw_ref[FFN_CONV - 1:FFN_CONV, cols]
            for j in range(FFN_CONV - 1):
                r0 = hist - (FFN_CONV - 1) + j
                y = y + buf[r0:r0 + tm, :] * cw_ref[j:j + 1, cols]
            halves.append(y + cb_ref[:, cols])
        hid = (_silu(halves[0]) * halves[1]).astype(BF16)
        part = _dot(hid, wdn_ref[f * tf:(f + 1) * tf, :])
        if f == 0:
            acc_ref[...] = part
        else:
            acc_ref[...] += part
    y = DEEPNORM_ALPHA * x + acc_ref[...]
    o_ref[...] = _layer_norm(y, g_ref[...], b_ref[...])


def _ffn(x2, w_up, conv_w, conv_b, w_down, g, b, seq, tm, tf):
    t = x2.shape[0]
    nf = D_FF // tf
    row = pl.BlockSpec((tm, D_MODEL), lambda i: (i, 0))
    return pl.pallas_call(
        functools.partial(_ffn_kernel, tm=tm, seq=seq, tf=tf),
        grid=(t // tm,),
        in_specs=[row, _const_spec(w_up.shape), _const_spec(conv_w.shape), _const_spec(conv_b.shape),
                  _const_spec(w_down.shape), _const_spec(g.shape), _const_spec(b.shape)],
        out_specs=row,
        out_shape=jax.ShapeDtypeStruct((t, D_MODEL), F32),
        scratch_shapes=[
            pltpu.VMEM((tm + SUBLANES, tf), F32), pltpu.VMEM((tm + SUBLANES, tf), F32),
            pltpu.VMEM((nf, SUBLANES, tf), F32), pltpu.VMEM((nf, SUBLANES, tf), F32),
            pltpu.VMEM((tm, D_MODEL), F32),
        ],
        compiler_params=pltpu.CompilerParams(dimension_semantics=("arbitrary",),
                                             vmem_limit_bytes=VMEM_LIMIT),
        name="ffn",
    )(x2, w_up, conv_w, conv_b, w_down, g, b)


def _rope_tables(seq):
    half = ROPE_DIMS // 2
    inv = ROPE_THETA ** (-jnp.arange(half, dtype=F32) / half)
    ang = jnp.arange(seq, dtype=jnp.int32).astype(F32)[:, None] * inv[None, :]
    cos, sin = jnp.cos(ang), jnp.sin(ang)
    rest = MOBA_DH - ROPE_DIMS
    cos_h = jnp.concatenate([cos, cos, jnp.ones((seq, rest), F32)], -1)
    sin_h = jnp.concatenate([-sin, sin, jnp.zeros((seq, rest), F32)], -1)
    reps = LANES // MOBA_DH
    return jnp.tile(cos_h, (1, reps)), jnp.tile(sin_h, (1, reps))


def _pad_lanes(v):
    return jnp.pad(v.astype(F32), (0, LANES - v.shape[0]))[None, :]


def kernel(x, w_in, gdn_conv_w, gdn_a_log, gdn_dt_bias, gdn_norm_g, w_out, ln1_g, ln1_b,
           w_up, ffn_conv_w, ffn_conv_b, w_down, ln2_g, ln2_b):
    batch, seq, d = x.shape
    t = batch * seq
    tm = min(512, seq)
    tl = min(512, seq)
    tf = 256
    cos_t, sin_t = _rope_tables(seq)
    g_off = 3 * GDN_W
    z_off = g_off + 2 * GDN_HEADS
    x2 = x.reshape(t, d)
    for l in range(DEPTH):
        w = w_in[l]
        w_main = jnp.concatenate([w[:, :g_off], w[:, z_off:]], axis=1).astype(BF16)
        w_gate = jnp.pad(w[:, g_off:z_off], ((0, 0), (0, LANES - 2 * GDN_HEADS))).astype(BF16)
        qkva, z, gates, qkvb = _inproj(x2, w_main, w_gate, tm)
        o_a = _gdn(qkva, z, gates, gdn_conv_w[l], _pad_lanes(gdn_a_log[l]), _pad_lanes(gdn_dt_bias[l]),
                   gdn_norm_g[l][None, :], batch, seq, tl)
        o_b = _moba(qkvb, cos_t, sin_t, batch, seq)
        x1 = _outproj(o_a, o_b, x2, w_out[l].astype(BF16), ln1_g[l][None, :], ln1_b[l][None, :], tm)
        x2 = _ffn(x1, w_up[l].astype(BF16), ffn_conv_w[l], ffn_conv_b[l][None, :],
                  w_down[l].astype(BF16), ln2_g[l][None, :], ln2_b[l][None, :], seq, tm, tf)
    return x2.reshape(batch, seq, d)
```

```python
import functools

import jax
import jax.numpy as jnp
import numpy as np
from jax import lax
from jax.experimental import pallas as pl
from jax.experimental.pallas import tpu as pltpu

F32 = jnp.float32
BF16 = jnp.bfloat16
HIGHEST = lax.Precision.HIGHEST

D_MODEL = 1024
DEPTH = 2
GDN_HEADS = 4
GDN_DK = 128
GDN_DV = 128
GDN_CONV = 4
GDN_CHUNK = 64
MOBA_HEADS = 8
MOBA_DH = 64
MOBA_BLOCK = 256
MOBA_TOPK = 3
ROPE_DIMS = MOBA_DH // 4
ROPE_THETA = 500000.0
GDN_W = GDN_HEADS * GDN_DK
MOBA_W = MOBA_HEADS * MOBA_DH
D_FF = 2816
FFN_CONV = 3
DEEPNORM_ALPHA = (2 * DEPTH) ** 0.25
LN_EPS = 1e-5
NORM_EPS = 1e-6

LANES = 128
SUBLANES = 8
NEG_BIG = -1e30
LOG2E = 1.4426950408889634
VMEM_LIMIT = 56 * 1024 * 1024


def _dot(a, b, precision=None):
    return jnp.dot(a, b, preferred_element_type=F32, precision=precision)


def _dot_nt(a, b, precision=None):
    return lax.dot_general(a, b, (((1,), (1,)), ((), ())), preferred_element_type=F32,
                           precision=precision)


def _dot_tn(a, b, precision=None):
    return lax.dot_general(a, b, (((0,), (0,)), ((), ())), preferred_element_type=F32,
                           precision=precision)


def _split2(x):
    hi = x.astype(BF16)
    return hi, (x - hi.astype(F32)).astype(BF16)


def _split3(x):
    hi = x.astype(BF16)
    r1 = x - hi.astype(F32)
    mid = r1.astype(BF16)
    return hi, mid, (r1 - mid.astype(F32)).astype(BF16)


def _dot3(a, b):
    return _dot(a[0], b[0]) + (_dot(a[0], b[1]) + _dot(a[1], b[0]))


def _sigmoid(x):
    return 1.0 / (1.0 + jnp.exp(-x))


def _silu(x):
    return x * _sigmoid(x)


def _layer_norm(y, g, b):
    mu = jnp.mean(y, axis=-1, keepdims=True)
    d = y - mu
    var = jnp.mean(d * d, axis=-1, keepdims=True)
    return d * lax.rsqrt(var + LN_EPS) * g + b


def _const_spec(shape):
    nd = len(shape)
    return pl.BlockSpec(shape, lambda *_: (0,) * nd, pipeline_mode=pl.Buffered(1))


def _inproj_kernel(x_ref, wm_ref, wg_ref, qkva_ref, z_ref, gates_ref, qkvb_ref):
    xb = x_ref[...].astype(BF16)
    a_w = 3 * GDN_W
    qkva_ref[...] = _dot(xb, wm_ref[:, 0:a_w])
    z_ref[...] = _dot(xb, wm_ref[:, a_w:a_w + GDN_W])
    qkvb_ref[...] = _dot(xb, wm_ref[:, a_w + GDN_W:])
    gates_ref[...] = _dot(xb, wg_ref[...])


def _inproj(x2, w_main, w_gate, tm):
    t = x2.shape[0]
    a_w = 3 * GDN_W
    b_w = 3 * MOBA_W
    row = lambda w: pl.BlockSpec((tm, w), lambda i: (i, 0))
    return pl.pallas_call(
        _inproj_kernel,
        grid=(t // tm,),
        in_specs=[row(D_MODEL), _const_spec(w_main.shape), _const_spec(w_gate.shape)],
        out_specs=[row(a_w), row(GDN_W), row(LANES), row(b_w)],
        out_shape=[jax.ShapeDtypeStruct((t, a_w), F32), jax.ShapeDtypeStruct((t, GDN_W), F32),
                   jax.ShapeDtypeStruct((t, LANES), F32), jax.ShapeDtypeStruct((t, b_w), F32)],
        compiler_params=pltpu.CompilerParams(dimension_semantics=("arbitrary",),
                                             vmem_limit_bytes=VMEM_LIMIT),
        name="inproj",
    )(x2, w_main, w_gate)


def _gdn_kernel(qkv_ref, z_ref, gates_ref, cw_ref, alog_ref, dtb_ref, ng_ref, o_ref,
                xbuf, gact, qs, ks, vs, u_s, w_s, qd_s, kd_s, at_s, egl_s, st_ref, o_s, *, tl, cpi):
    c = GDN_CHUNK
    nc = tl // c
    hist = SUBLANES

    @pl.when(pl.program_id(1) == 0)
    def _():
        xbuf[0:hist, :] = jnp.zeros((hist, 3 * GDN_W), F32)
        st_ref[...] = jnp.zeros_like(st_ref)

    xbuf[hist:hist + tl, :] = qkv_ref[...]

    gr = gates_ref[...]
    lane = lax.broadcasted_iota(jnp.int32, gr.shape, 1)
    sp_in = gr + dtb_ref[...]
    softplus = jnp.maximum(sp_in, 0.0) + jnp.log(1.0 + jnp.exp(-jnp.abs(sp_in)))
    gact[...] = jnp.where(lane < GDN_HEADS, -jnp.exp(alog_ref[...]) * softplus, _sigmoid(gr))

    for part, dst in enumerate((qs, ks, vs)):
        for h in range(GDN_HEADS):
            c0 = part * GDN_W + h * GDN_DK
            acc = None
            for j in range(GDN_CONV):
                r0 = hist - (GDN_CONV - 1) + j
                term = xbuf[r0:r0 + tl, c0:c0 + GDN_DK] * cw_ref[j:j + 1, c0:c0 + GDN_DK]
                acc = term if acc is None else acc + term
            y = _silu(acc)
            if part < 2:
                y = y * lax.rsqrt(jnp.sum(y * y, axis=-1, keepdims=True) + NORM_EPS)
            if part == 0:
                y = y * (GDN_DK ** -0.5)
            dst[:, h * GDN_DK:(h + 1) * GDN_DK] = y
    xbuf[0:hist, :] = xbuf[tl:tl + hist, :]

    ii = lax.broadcasted_iota(jnp.int32, (c, LANES), 0)
    lane = lax.broadcasted_iota(jnp.int32, (c, LANES), 1)
    jj = lane % c
    left = lane < c
    mask_l = jnp.where(left, 1.0, 0.0).astype(BF16)
    mask_r = jnp.where(left, 0.0, 1.0).astype(BF16)
    eye_p = jnp.where(ii == jj, 1.0, 0.0)
    i3 = lax.broadcasted_iota(jnp.int32, (c, 3 * c), 0)
    t3 = lax.broadcasted_iota(jnp.int32, (c, 3 * c), 1) % c
    ltri3 = jnp.where(i3 >= t3, 1.0, 0.0).astype(BF16)

    def blockdiag(p):
        return jnp.concatenate([p * mask_l, p * mask_r], axis=0)

    def blockdiag_wide(x):
        n = x.shape[1] // 2
        z = jnp.zeros((c, n), BF16)
        return jnp.concatenate([jnp.concatenate([x[:, :n], z], axis=1),
                                jnp.concatenate([z, x[:, n:]], axis=1)], axis=0)

    def dot3(a, b_hi, b_lo):
        lhs = jnp.concatenate([a[0], a[1], a[0]], axis=1)
        return _dot(lhs, jnp.concatenate([b_hi, b_hi, b_lo], axis=0))

    def pair_gates(rows, p):
        gblk = gact[rows, :]
        gc_all = _dot(ltri3, jnp.concatenate(_split3(gblk), axis=0))
        bc = lambda a, l: jnp.broadcast_to(a[:, l:l + 1], (c, LANES))
        gc_cat = jnp.concatenate([bc(gc_all, 2 * p), bc(gc_all, 2 * p + 1)], axis=1)
        b_cat = jnp.concatenate([bc(gblk, GDN_HEADS + 2 * p), bc(gblk, GDN_HEADS + 2 * p + 1)], axis=1)
        return gc_cat, b_cat

    def phase1(it, carry):
        chains = [(it * cpi + cc, p) for cc in range(cpi) for p in range(GDN_HEADS // 2)]
        t_mats, pws, rhs = [], [], []
        for ci, p in chains:
            rows = pl.ds(pl.multiple_of(ci * c, c), c)
            cols = slice(p * 2 * GDN_DK, (p + 1) * 2 * GDN_DK)
            gc_cat, b_cat = pair_gates(rows, p)
            gc_p = jnp.where(left, gc_cat[:, :LANES], gc_cat[:, LANES:])
            gcr_p = jnp.sum(jnp.where(ii == jj, gc_p, 0.0), axis=0, keepdims=True)
            decay = jnp.exp(jnp.minimum(gc_p - gcr_p, 0.0))
            q = qs[rows, cols]
            k = ks[rows, cols]
            kb = k * b_cat
            kq = _dot_nt(jnp.concatenate([kb.astype(BF16), q.astype(BF16)], axis=0),
                         blockdiag_wide(k.astype(BF16)))
            a_mat = jnp.where(ii > jj, kq[:c] * decay, 0.0)
            at_s[p, rows, :] = jnp.where(ii >= jj, kq[c:] * decay, 0.0).astype(BF16)
            gl = gc_cat[c - 1:c, :]
            egc = jnp.exp(gc_cat)
            qd_s[p, rows, :] = (q * egc).astype(BF16)
            kd_s[p, rows, :] = (k * jnp.exp(gl - gc_cat)).astype(BF16)
            egl_s[p * nc + ci] = jnp.broadcast_to(jnp.exp(gl), (SUBLANES, 2 * LANES))
            vb = vs[rows, cols] * b_cat
            ke = kb * egc
            x = _split2(jnp.concatenate([vb[:, :GDN_DV], ke[:, :GDN_DK], vb[:, GDN_DV:], ke[:, GDN_DK:]],
                                        axis=1))
            rhs.append((blockdiag_wide(x[0]), blockdiag_wide(x[1])))
            t_mats.append(eye_p - a_mat)
            pws.append(_split2(a_mat))
        for _ in range(5):
            pws = [_split2(dot3(pw, blockdiag(pw[0]), blockdiag(pw[1]))) for pw in pws]
            t_mats = [t + dot3(_split2(t), blockdiag(pw[0]), blockdiag(pw[1]))
                      for t, pw in zip(t_mats, pws)]
        for (ci, p), t_mat, x in zip(chains, t_mats, rhs):
            rows = pl.ds(pl.multiple_of(ci * c, c), c)
            sol = dot3(_split2(t_mat), x[0], x[1])
            u_s[p, rows, :] = jnp.concatenate([sol[:, 0:128], sol[:, 256:384]], axis=1)
            w_s[p, rows, :] = jnp.concatenate([sol[:, 128:256], sol[:, 384:512]], axis=1).astype(BF16)
        return carry

    lax.fori_loop(0, nc // cpi, phase1, 0)

    zs = jnp.zeros((GDN_DK, GDN_DV), BF16)

    def phase2(ci, carry):
        rows = pl.ds(pl.multiple_of(ci * c, c), c)
        pairs = range(GDN_HEADS // 2)
        sts = [(st_ref[2 * p], st_ref[2 * p + 1]) for p in pairs]
        rs = []
        for p in pairs:
            bds = jnp.concatenate([jnp.concatenate([sts[p][0].astype(BF16), zs], axis=1),
                                   jnp.concatenate([zs, sts[p][1].astype(BF16)], axis=1)], axis=0)
            rs.append(_dot(jnp.concatenate([w_s[p, rows, :], qd_s[p, rows, :]], axis=0), bds))
        vnbs = [(u_s[p, rows, :] - rs[p][:c]).astype(BF16) for p in pairs]
        upds = [_dot_tn(kd_s[p, rows, :], vnbs[p]) for p in pairs]
        for p in pairs:
            o_s[rows, p * 2 * GDN_DV:(p + 1) * 2 * GDN_DV] = (
                rs[p][c:] + _dot(at_s[p, rows, :], blockdiag_wide(vnbs[p])))
        for p in pairs:
            egl = egl_s[p * nc + ci][0:1, :]
            st_ref[2 * p] = sts[p][0] * egl[:, :GDN_DV] + upds[p][:GDN_DK, :GDN_DV]
            st_ref[2 * p + 1] = sts[p][1] * egl[:, GDN_DV:] + upds[p][GDN_DK:, GDN_DV:]
        return carry

    lax.fori_loop(0, nc, phase2, 0)

    for h in range(GDN_HEADS):
        cols = slice(h * GDN_DV, (h + 1) * GDN_DV)
        o = o_s[:, cols]
        o = o * lax.rsqrt(jnp.mean(o * o, axis=-1, keepdims=True) + NORM_EPS) * ng_ref[...]
        o_ref[:, cols] = (o * _silu(z_ref[:, cols])).astype(o_ref.dtype)


def _gdn(qkva, z, gates, conv_w, alog_row, dtb_row, ng_row, batch, seq, tl):
    t = batch * seq
    nt = seq // tl
    nc = tl // GDN_CHUNK
    a_w = 3 * GDN_W
    row = lambda w: pl.BlockSpec((tl, w), lambda b, i: (b * nt + i, 0))
    npair = GDN_HEADS // 2
    pshape = (npair, tl, 2 * GDN_DK)
    return pl.pallas_call(
        functools.partial(_gdn_kernel, tl=tl, cpi=min(4, nc)),
        grid=(batch, nt),
        in_specs=[row(a_w), row(GDN_W), row(LANES), _const_spec(conv_w.shape),
                  _const_spec(alog_row.shape), _const_spec(dtb_row.shape), _const_spec(ng_row.shape)],
        out_specs=row(GDN_W),
        out_shape=jax.ShapeDtypeStruct((t, GDN_W), BF16),
        scratch_shapes=[
            pltpu.VMEM((tl + SUBLANES, a_w), F32),
            pltpu.VMEM((tl, LANES), F32),
            pltpu.VMEM((tl, GDN_W), F32),
            pltpu.VMEM((tl, GDN_W), F32),
            pltpu.VMEM((tl, GDN_W), F32),
            pltpu.VMEM(pshape, F32),
            pltpu.VMEM(pshape, BF16),
            pltpu.VMEM(pshape, BF16),
            pltpu.VMEM(pshape, BF16),
            pltpu.VMEM((npair, tl, 2 * GDN_CHUNK), BF16),
            pltpu.VMEM((npair * nc, SUBLANES, 2 * LANES), F32),
            pltpu.VMEM((GDN_HEADS, GDN_DK, GDN_DV), F32),
            pltpu.VMEM((tl, GDN_W), F32),
        ],
        compiler_params=pltpu.CompilerParams(dimension_semantics=("arbitrary", "arbitrary"),
                                             vmem_limit_bytes=VMEM_LIMIT),
        name="gdn",
    )(qkva, z, gates, conv_w, alog_row, dtb_row, ng_row)


def _rope(x, cos_t, sin_t):
    half = ROPE_DIMS // 2
    lane = lax.broadcasted_iota(jnp.int32, x.shape, 1)
    up = pltpu.roll(x, LANES - half, axis=1)
    dn = pltpu.roll(x, half, axis=1)
    partner = jnp.where((lane % MOBA_DH) < half, up, dn)
    return x * cos_t + partner * sin_t


def _moba_kernel(q_ref, k_ref, v_ref, cos_ref, sin_ref, o_ref, kr_s, vb_s, kmean_s,
                 qa_s, s_all, m_run, l_acc, acc_s, *, nb):
    blk = MOBA_BLOCK
    lane = lax.broadcasted_iota(jnp.int32, (blk, LANES), 1)
    head_a = lane < MOBA_DH
    scale = MOBA_DH ** -0.5

    kmean_s[...] = jnp.zeros_like(kmean_s)

    def prep(n, carry):
        rows = pl.ds(pl.multiple_of(n * blk, blk), blk)
        kr = _rope(k_ref[rows, :], cos_ref[rows, :], sin_ref[rows, :])
        kr_s[rows, :] = kr.astype(BF16)
        vb_s[rows, :] = v_ref[rows, :].astype(BF16)
        kmean_s[pl.ds(n, 1), :] = jnp.mean(kr, axis=0, keepdims=True)
        return carry

    lax.fori_loop(0, nb, prep, 0)

    nbp = -(-nb // SUBLANES) * SUBLANES
    kmean = kmean_s[0:nbp, :]
    brow = lax.broadcasted_iota(jnp.int32, (nbp, blk), 0)

    def gating(it, carry):
        items = []
        for d in range(2):
            i = 2 * it + d
            qrows = pl.ds(pl.multiple_of(i * blk, blk), blk)
            q = _rope(q_ref[qrows, :], cos_ref[qrows, :], sin_ref[qrows, :])
            for h, qh in enumerate((jnp.where(head_a, q, 0.0), jnp.where(head_a, 0.0, q))):
                qa_s[h, qrows, 0:LANES] = (qh * scale).astype(BF16)
                items.append((i, h, qrows, qh))
        gs = [jnp.where(brow < i, _dot_nt(kmean, qh, HIGHEST), -jnp.inf) for i, _, _, qh in items]
        sels = [brow == i for i, _, _, _ in items]
        for r in range(MOBA_TOPK):
            for n, (i, _, _, _) in enumerate(items):
                g = gs[n]
                m = jnp.max(g, axis=0, keepdims=True)
                first = jnp.min(jnp.where(g == m, brow, nbp), axis=0, keepdims=True)
                pick = brow == first
                sels[n] = jnp.logical_or(sels[n], jnp.logical_and(pick, r < i))
                gs[n] = jnp.where(pick, -jnp.inf, g)
        for (i, h, qrows, _), sel in zip(items, sels):
            bias_t = jnp.concatenate([jnp.where(sel, 0.0, NEG_BIG),
                                      jnp.zeros((LANES - nbp, blk), F32)], axis=0)
            qa_s[h, qrows, LANES:] = bias_t.T.astype(BF16)
        return carry

    lax.fori_loop(0, nb // 2, gating, 0)

    sb = 2 * blk
    causal = (lax.broadcasted_iota(jnp.int32, (sb, sb), 1)
              <= lax.broadcasted_iota(jnp.int32, (sb, sb), 0))
    lane2 = lax.broadcasted_iota(jnp.int32, (sb, LANES), 1)
    half2 = lax.broadcasted_iota(jnp.int32, (sb, LANES), 0) // blk
    head_a2 = lane2 < MOBA_DH
    ntile = sb // LANES

    def k_aug(j):
        rows = pl.ds(pl.multiple_of(j * sb, sb), sb)
        onehot = jnp.where(lane2 == 2 * j + half2, 1.0, 0.0).astype(BF16)
        return jnp.concatenate([kr_s[rows, :], onehot], axis=1)

    def row_max(s):
        parts = [s[:, t * LANES:(t + 1) * LANES] for t in range(ntile)]
        return jnp.maximum(jnp.maximum(parts[0], parts[1]), jnp.maximum(parts[2], parts[3]))

    def qpair(qi, carry):
        qrows = pl.ds(pl.multiple_of(qi * sb, sb), sb)

        for h in range(2):
            m_run[h] = jnp.full((sb, LANES), NEG_BIG, F32)

        def pass1(j, c):
            ka = k_aug(j)
            for h in range(2):
                s = _dot_nt(qa_s[h, qrows, :], ka) * LOG2E
                s_all[h, j] = s
                m_run[h] = jnp.maximum(m_run[h], row_max(s))
            return c

        lax.fori_loop(0, qi, pass1, 0)
        ka = k_aug(qi)
        for h in range(2):
            s = jnp.where(causal, _dot_nt(qa_s[h, qrows, :], ka) * LOG2E, NEG_BIG)
            s_all[h, qi] = s
            m_run[h] = jnp.maximum(m_run[h], row_max(s))

        for h in range(2):
            m_run[h] = jnp.broadcast_to(jnp.max(m_run[h], axis=-1, keepdims=True), (sb, LANES))
            l_acc[h] = jnp.zeros((sb, LANES), F32)
            acc_s[h] = jnp.zeros((sb, LANES), F32)

        def pass2(j, c):
            v2 = vb_s[pl.ds(pl.multiple_of(j * sb, sb), sb), :]
            for h in range(2):
                mb = m_run[h]
                ps = [jnp.exp2(s_all[h, j, :, t * LANES:(t + 1) * LANES] - mb) for t in range(ntile)]
                l_acc[h] += (ps[0] + ps[1]) + (ps[2] + ps[3])
                acc_s[h] += _dot(jnp.concatenate(ps, axis=1).astype(BF16), v2)
            return c

        lax.fori_loop(0, qi + 1, pass2, 0)
        o_a = acc_s[0] / jnp.sum(l_acc[0], axis=-1, keepdims=True)
        o_b = acc_s[1] / jnp.sum(l_acc[1], axis=-1, keepdims=True)
        o_ref[qrows, :] = jnp.where(head_a2, o_a, o_b).astype(o_ref.dtype)
        return carry

    lax.fori_loop(0, nb // 2, qpair, 0)


def _moba(qkvb, cos_t, sin_t, batch, seq):
    t = batch * seq
    nb = seq // MOBA_BLOCK
    assert nb % 2 == 0, "query and key blocks are processed in pairs"
    sb = 2 * MOBA_BLOCK
    npair = MOBA_W // LANES
    col = lambda off: pl.BlockSpec((seq, LANES), lambda b, p: (b, off + p))
    return pl.pallas_call(
        functools.partial(_moba_kernel, nb=nb),
        grid=(batch, npair),
        in_specs=[col(0), col(npair), col(2 * npair), _const_spec(cos_t.shape), _const_spec(sin_t.shape)],
        out_specs=col(0),
        out_shape=jax.ShapeDtypeStruct((t, MOBA_W), BF16),
        scratch_shapes=[
            pltpu.VMEM((seq, LANES), BF16),
            pltpu.VMEM((seq, LANES), BF16),
            pltpu.VMEM((LANES, LANES), F32),
            pltpu.VMEM((2, seq, 2 * LANES), BF16),
            pltpu.VMEM((2, nb // 2, sb, sb), F32),
            pltpu.VMEM((2, sb, LANES), F32),
            pltpu.VMEM((2, sb, LANES), F32),
            pltpu.VMEM((2, sb, LANES), F32),
        ],
        compiler_params=pltpu.CompilerParams(dimension_semantics=("arbitrary", "arbitrary"),
                                             vmem_limit_bytes=VMEM_LIMIT),
        name="moba",
    )(qkvb, qkvb, qkvb, cos_t, sin_t)


def _outproj_kernel(oa_ref, ob_ref, x_ref, w_ref, g_ref, b_ref, o_ref):
    mix = _dot(oa_ref[...], w_ref[0:GDN_W, :]) + _dot(ob_ref[...], w_ref[GDN_W:, :])
    y = DEEPNORM_ALPHA * x_ref[...] + mix
    o_ref[...] = _layer_norm(y, g_ref[...], b_ref[...])


def _outproj(o_a, o_b, x2, w_out, g, b, tm):
    t = x2.shape[0]
    row = lambda w: pl.BlockSpec((tm, w), lambda i: (i, 0))
    return pl.pallas_call(
        _outproj_kernel,
        grid=(t // tm,),
        in_specs=[row(GDN_W), row(MOBA_W), row(D_MODEL), _const_spec(w_out.shape),
                  _const_spec(g.shape), _const_spec(b.shape)],
        out_specs=row(D_MODEL),
        out_shape=jax.ShapeDtypeStruct((t, D_MODEL), F32),
        compiler_params=pltpu.CompilerParams(dimension_semantics=("arbitrary",),
                                             vmem_limit_bytes=VMEM_LIMIT),
        name="outproj",
    )(o_a, o_b, x2, w_out, g, b)


def _ffn_kernel(x_ref, wup_ref, cw_ref, cb_ref, wdn_ref, g_ref, b_ref, o_ref,
                buf_g, buf_v, carry_g, carry_v, hid_s, *, tm, seq, tf):
    nf = D_FF // tf
    hist = SUBLANES
    seq_start = (pl.program_id(0) * tm) % seq == 0
    x = x_ref[...]
    xb = x.astype(BF16)
    for f in range(nf):
        halves = []
        for buf, carry, off in ((buf_g, carry_g, 0), (buf_v, carry_v, D_FF)):
            cols = slice(off + f * tf, off + (f + 1) * tf)
            up = _dot(xb, wup_ref[:, cols])
            buf[0:hist, :] = jnp.where(seq_start, 0.0, carry[f])
            buf[hist:hist + tm, :] = up
            carry[f] = up[tm - hist:tm, :]
            y = up * cw_ref[FFN_CONV - 1:FFN_CONV, cols]
            for j in range(FFN_CONV - 1):
                r0 = hist - (FFN_CONV - 1) + j
                y = y + buf[r0:r0 + tm, :] * cw_ref[j:j + 1, cols]
            halves.append(y + cb_ref[:, cols])
        hid_s[:, f * tf:(f + 1) * tf] = (_silu(halves[0]) * halves[1]).astype(BF16)
    y = DEEPNORM_ALPHA * x + _dot(hid_s[...], wdn_ref[...])
    o_ref[...] = _layer_norm(y, g_ref[...], b_ref[...])


def _ffn(x2, w_up, conv_w, conv_b, w_down, g, b, seq, tm, tf):
    t = x2.shape[0]
    nf = D_FF // tf
    row = pl.BlockSpec((tm, D_MODEL), lambda i: (i, 0))
    return pl.pallas_call(
        functools.partial(_ffn_kernel, tm=tm, seq=seq, tf=tf),
        grid=(t // tm,),
        in_specs=[row, _const_spec(w_up.shape), _const_spec(conv_w.shape), _const_spec(conv_b.shape),
                  _const_spec(w_down.shape), _const_spec(g.shape), _const_spec(b.shape)],
        out_specs=row,
        out_shape=jax.ShapeDtypeStruct((t, D_MODEL), F32),
        scratch_shapes=[
            pltpu.VMEM((tm + SUBLANES, tf), F32), pltpu.VMEM((tm + SUBLANES, tf), F32),
            pltpu.VMEM((nf, SUBLANES, tf), F32), pltpu.VMEM((nf, SUBLANES, tf), F32),
            pltpu.VMEM((tm, D_FF), BF16),
        ],
        compiler_params=pltpu.CompilerParams(dimension_semantics=("arbitrary",),
                                             vmem_limit_bytes=VMEM_LIMIT),
        name="ffn",
    )(x2, w_up, conv_w, conv_b, w_down, g, b)


def _rope_tables(seq):
    half = ROPE_DIMS // 2
    inv = ROPE_THETA ** (-jnp.arange(half, dtype=F32) / half)
    ang = jnp.arange(seq, dtype=jnp.int32).astype(F32)[:, None] * inv[None, :]
    cos, sin = jnp.cos(ang), jnp.sin(ang)
    rest = MOBA_DH - ROPE_DIMS
    cos_h = jnp.concatenate([cos, cos, jnp.ones((seq, rest), F32)], -1)
    sin_h = jnp.concatenate([-sin, sin, jnp.zeros((seq, rest), F32)], -1)
    reps = LANES // MOBA_DH
    return jnp.tile(cos_h, (1, reps)), jnp.tile(sin_h, (1, reps))


def _pad_lanes(v):
    return jnp.pad(v.astype(F32), (0, LANES - v.shape[0]))[None, :]


def kernel(x, w_in, gdn_conv_w, gdn_a_log, gdn_dt_bias, gdn_norm_g, w_out, ln1_g, ln1_b,
           w_up, ffn_conv_w, ffn_conv_b, w_down, ln2_g, ln2_b):
    batch, seq, d = x.shape
    t = batch * seq
    tm = min(512, seq)
    tm_ffn = min(1024, seq)
    tl = min(512, seq)
    tf = 256
    cos_t, sin_t = _rope_tables(seq)
    g_off = 3 * GDN_W
    z_off = g_off + 2 * GDN_HEADS
    x2 = x.reshape(t, d)
    for l in range(DEPTH):
        w = w_in[l]
        w_main = jnp.concatenate([w[:, :g_off], w[:, z_off:]], axis=1).astype(BF16)
        w_gate = jnp.pad(w[:, g_off:z_off], ((0, 0), (0, LANES - 2 * GDN_HEADS))).astype(BF16)
        qkva, z, gates, qkvb = _inproj(x2, w_main, w_gate, tm)
        o_a = _gdn(qkva, z, gates, gdn_conv_w[l], _pad_lanes(gdn_a_log[l]), _pad_lanes(gdn_dt_bias[l]),
                   gdn_norm_g[l][None, :], batch, seq, tl)
        o_b = _moba(qkvb, cos_t, sin_t, batch, seq)
        x1 = _outproj(o_a, o_b, x2, w_out[l].astype(BF16), ln1_g[l][None, :], ln1_b[l][None, :], tm)
        x2 = _ffn(x1, w_up[l].astype(BF16), ffn_conv_w[l], ffn_conv_b[l][None, :],
                  w_down[l].astype(BF16), ln2_g[l][None, :], ln2_b[l][None, :], seq, tm_ffn, tf)
    return x2.reshape(batch, seq, d)
```

```python
import functools

import jax
import jax.numpy as jnp
import numpy as np
from jax import lax
from jax.experimental import pallas as pl
from jax.experimental.pallas import tpu as pltpu

F32 = jnp.float32
BF16 = jnp.bfloat16
HIGHEST = lax.Precision.HIGHEST

D_MODEL = 1024
DEPTH = 2
GDN_HEADS = 4
GDN_DK = 128
GDN_DV = 128
GDN_CONV = 4
GDN_CHUNK = 64
MOBA_HEADS = 8
MOBA_DH = 64
MOBA_BLOCK = 256
MOBA_TOPK = 3
ROPE_DIMS = MOBA_DH // 4
ROPE_THETA = 500000.0
GDN_W = GDN_HEADS * GDN_DK
MOBA_W = MOBA_HEADS * MOBA_DH
D_FF = 2816
FFN_CONV = 3
DEEPNORM_ALPHA = (2 * DEPTH) ** 0.25
LN_EPS = 1e-5
NORM_EPS = 1e-6

LANES = 128
SUBLANES = 8
NEG_BIG = -1e30
LOG2E = 1.4426950408889634
VMEM_LIMIT = 56 * 1024 * 1024


def _dot(a, b, precision=None):
    return jnp.dot(a, b, preferred_element_type=F32, precision=precision)


def _dot_nt(a, b, precision=None):
    return lax.dot_general(a, b, (((1,), (1,)), ((), ())), preferred_element_type=F32,
                           precision=precision)


def _dot_tn(a, b, precision=None):
    return lax.dot_general(a, b, (((0,), (0,)), ((), ())), preferred_element_type=F32,
                           precision=precision)


def _split2(x):
    hi = x.astype(BF16)
    return hi, (x - hi.astype(F32)).astype(BF16)


def _split3(x):
    hi = x.astype(BF16)
    r1 = x - hi.astype(F32)
    mid = r1.astype(BF16)
    return hi, mid, (r1 - mid.astype(F32)).astype(BF16)


def _dot3(a, b):
    return _dot(a[0], b[0]) + (_dot(a[0], b[1]) + _dot(a[1], b[0]))


def _sigmoid(x):
    return 1.0 / (1.0 + jnp.exp(-x))


def _silu(x):
    return x * _sigmoid(x)


def _layer_norm(y, g, b):
    mu = jnp.mean(y, axis=-1, keepdims=True)
    d = y - mu
    var = jnp.mean(d * d, axis=-1, keepdims=True)
    return d * lax.rsqrt(var + LN_EPS) * g + b


def _const_spec(shape):
    nd = len(shape)
    return pl.BlockSpec(shape, lambda *_: (0,) * nd, pipeline_mode=pl.Buffered(1))


def _inproj_kernel(x_ref, wm_ref, wg_ref, qkva_ref, z_ref, gates_ref, qkvb_ref):
    xb = x_ref[...].astype(BF16)
    a_w = 3 * GDN_W
    qkva_ref[...] = _dot(xb, wm_ref[:, 0:a_w])
    z_ref[...] = _dot(xb, wm_ref[:, a_w:a_w + GDN_W])
    qkvb_ref[...] = _dot(xb, wm_ref[:, a_w + GDN_W:])
    gates_ref[...] = _dot(xb, wg_ref[...])


def _inproj(x2, w_main, w_gate, tm):
    t = x2.shape[0]
    a_w = 3 * GDN_W
    b_w = 3 * MOBA_W
    row = lambda w: pl.BlockSpec((tm, w), lambda i: (i, 0))
    return pl.pallas_call(
        _inproj_kernel,
        grid=(t // tm,),
        in_specs=[row(D_MODEL), _const_spec(w_main.shape), _const_spec(w_gate.shape)],
        out_specs=[row(a_w), row(GDN_W), row(LANES), row(b_w)],
        out_shape=[jax.ShapeDtypeStruct((t, a_w), F32), jax.ShapeDtypeStruct((t, GDN_W), F32),
                   jax.ShapeDtypeStruct((t, LANES), F32), jax.ShapeDtypeStruct((t, b_w), F32)],
        compiler_params=pltpu.CompilerParams(dimension_semantics=("arbitrary",),
                                             vmem_limit_bytes=VMEM_LIMIT),
        name="inproj",
    )(x2, w_main, w_gate)


def _gdn_kernel(qkv_ref, z_ref, gates_ref, cw_ref, alog_ref, dtb_ref, ng_ref, o_ref,
                xbuf, gact, qs, ks, vs, u_s, w_s, qd_s, kd_s, at_s, egl_s, st_ref, o_s, *, tl, cpi):
    c = GDN_CHUNK
    nc = tl // c
    hist = SUBLANES

    @pl.when(pl.program_id(1) == 0)
    def _():
        xbuf[0:hist, :] = jnp.zeros((hist, 3 * GDN_W), F32)
        st_ref[...] = jnp.zeros_like(st_ref)

    xbuf[hist:hist + tl, :] = qkv_ref[...]

    gr = gates_ref[...]
    lane = lax.broadcasted_iota(jnp.int32, gr.shape, 1)
    sp_in = gr + dtb_ref[...]
    softplus = jnp.maximum(sp_in, 0.0) + jnp.log(1.0 + jnp.exp(-jnp.abs(sp_in)))
    gact[...] = jnp.where(lane < GDN_HEADS, -jnp.exp(alog_ref[...]) * softplus, _sigmoid(gr))

    for part, dst in enumerate((qs, ks, vs)):
        for h in range(GDN_HEADS):
            c0 = part * GDN_W + h * GDN_DK
            acc = None
            for j in range(GDN_CONV):
                r0 = hist - (GDN_CONV - 1) + j
                term = xbuf[r0:r0 + tl, c0:c0 + GDN_DK] * cw_ref[j:j + 1, c0:c0 + GDN_DK]
                acc = term if acc is None else acc + term
            y = _silu(acc)
            if part < 2:
                y = y * lax.rsqrt(jnp.sum(y * y, axis=-1, keepdims=True) + NORM_EPS)
            if part == 0:
                y = y * (GDN_DK ** -0.5)
            dst[:, h * GDN_DK:(h + 1) * GDN_DK] = y
    xbuf[0:hist, :] = xbuf[tl:tl + hist, :]

    ii = lax.broadcasted_iota(jnp.int32, (c, LANES), 0)
    lane = lax.broadcasted_iota(jnp.int32, (c, LANES), 1)
    jj = lane % c
    left = lane < c
    mask_l = jnp.where(left, 1.0, 0.0).astype(BF16)
    mask_r = jnp.where(left, 0.0, 1.0).astype(BF16)
    eye_p = jnp.where(ii == jj, 1.0, 0.0)
    i3 = lax.broadcasted_iota(jnp.int32, (c, 3 * c), 0)
    t3 = lax.broadcasted_iota(jnp.int32, (c, 3 * c), 1) % c
    ltri3 = jnp.where(i3 >= t3, 1.0, 0.0).astype(BF16)

    def blockdiag(p):
        return jnp.concatenate([p * mask_l, p * mask_r], axis=0)

    def blockdiag_wide(x):
        n = x.shape[1] // 2
        z = jnp.zeros((c, n), BF16)
        return jnp.concatenate([jnp.concatenate([x[:, :n], z], axis=1),
                                jnp.concatenate([z, x[:, n:]], axis=1)], axis=0)

    def dot3(a, b_hi, b_lo):
        lhs = jnp.concatenate([a[0], a[1], a[0]], axis=1)
        return _dot(lhs, jnp.concatenate([b_hi, b_hi, b_lo], axis=0))

    def pair_gates(rows, p):
        gblk = gact[rows, :]
        gc_all = _dot(ltri3, jnp.concatenate(_split3(gblk), axis=0))
        bc = lambda a, l: jnp.broadcast_to(a[:, l:l + 1], (c, LANES))
        gc_cat = jnp.concatenate([bc(gc_all, 2 * p), bc(gc_all, 2 * p + 1)], axis=1)
        b_cat = jnp.concatenate([bc(gblk, GDN_HEADS + 2 * p), bc(gblk, GDN_HEADS + 2 * p + 1)], axis=1)
        return gc_cat, b_cat

    def phase1(it, carry):
        chains = [(it * cpi + cc, p) for cc in range(cpi) for p in range(GDN_HEADS // 2)]
        t_mats, pws, rhs = [], [], []
        for ci, p in chains:
            rows = pl.ds(pl.multiple_of(ci * c, c), c)
            cols = slice(p * 2 * GDN_DK, (p + 1) * 2 * GDN_DK)
            gc_cat, b_cat = pair_gates(rows, p)
            gc_p = jnp.where(left, gc_cat[:, :LANES], gc_cat[:, LANES:])
            gcr_p = jnp.sum(jnp.where(ii == jj, gc_p, 0.0), axis=0, keepdims=True)
            decay = jnp.exp(jnp.minimum(gc_p - gcr_p, 0.0))
            q = qs[rows, cols]
            k = ks[rows, cols]
            kb = k * b_cat
            kq = _dot_nt(jnp.concatenate([kb.astype(BF16), q.astype(BF16)], axis=0),
                         blockdiag_wide(k.astype(BF16)))
            a_mat = jnp.where(ii > jj, kq[:c] * decay, 0.0)
            at_s[p, rows, :] = jnp.where(ii >= jj, kq[c:] * decay, 0.0).astype(BF16)
            gl = gc_cat[c - 1:c, :]
            egc = jnp.exp(gc_cat)
            qd_s[p, rows, :] = (q * egc).astype(BF16)
            kd_s[p, rows, :] = (k * jnp.exp(gl - gc_cat)).astype(BF16)
            egl_s[p * nc + ci] = jnp.broadcast_to(jnp.exp(gl), (SUBLANES, 2 * LANES))
            vb = vs[rows, cols] * b_cat
            ke = kb * egc
            x = _split2(jnp.concatenate([vb[:, :GDN_DV], ke[:, :GDN_DK], vb[:, GDN_DV:], ke[:, GDN_DK:]],
                                        axis=1))
            rhs.append((blockdiag_wide(x[0]), blockdiag_wide(x[1])))
            t_mats.append(eye_p - a_mat)
            pws.append(_split2(a_mat))
        for _ in range(5):
            pws = [_split2(dot3(pw, blockdiag(pw[0]), blockdiag(pw[1]))) for pw in pws]
            t_mats = [t + dot3(_split2(t), blockdiag(pw[0]), blockdiag(pw[1]))
                      for t, pw in zip(t_mats, pws)]
        for (ci, p), t_mat, x in zip(chains, t_mats, rhs):
            rows = pl.ds(pl.multiple_of(ci * c, c), c)
            sol = dot3(_split2(t_mat), x[0], x[1])
            u_s[p, rows, :] = jnp.concatenate([sol[:, 0:128], sol[:, 256:384]], axis=1)
            w_s[p, rows, :] = jnp.concatenate([sol[:, 128:256], sol[:, 384:512]], axis=1).astype(BF16)
        return carry

    lax.fori_loop(0, nc // cpi, phase1, 0)

    zs = jnp.zeros((GDN_DK, GDN_DV), BF16)

    def phase2(ci, carry):
        rows = pl.ds(pl.multiple_of(ci * c, c), c)
        pairs = range(GDN_HEADS // 2)
        sts = [(st_ref[2 * p], st_ref[2 * p + 1]) for p in pairs]
        rs = []
        for p in pairs:
            bds = jnp.concatenate([jnp.concatenate([sts[p][0].astype(BF16), zs], axis=1),
                                   jnp.concatenate([zs, sts[p][1].astype(BF16)], axis=1)], axis=0)
            rs.append(_dot(jnp.concatenate([w_s[p, rows, :], qd_s[p, rows, :]], axis=0), bds))
        vnbs = [(u_s[p, rows, :] - rs[p][:c]).astype(BF16) for p in pairs]
        upds = [_dot_tn(kd_s[p, rows, :], vnbs[p]) for p in pairs]
        for p in pairs:
            o_s[rows, p * 2 * GDN_DV:(p + 1) * 2 * GDN_DV] = (
                rs[p][c:] + _dot(at_s[p, rows, :], blockdiag_wide(vnbs[p])))
        for p in pairs:
            egl = egl_s[p * nc + ci][0:1, :]
            st_ref[2 * p] = sts[p][0] * egl[:, :GDN_DV] + upds[p][:GDN_DK, :GDN_DV]
            st_ref[2 * p + 1] = sts[p][1] * egl[:, GDN_DV:] + upds[p][GDN_DK:, GDN_DV:]
        return carry

    lax.fori_loop(0, nc, phase2, 0)

    for h in range(GDN_HEADS):
        cols = slice(h * GDN_DV, (h + 1) * GDN_DV)
        o = o_s[:, cols]
        o = o * lax.rsqrt(jnp.mean(o * o, axis=-1, keepdims=True) + NORM_EPS) * ng_ref[...]
        o_ref[:, cols] = (o * _silu(z_ref[:, cols])).astype(o_ref.dtype)


def _gdn(qkva, z, gates, conv_w, alog_row, dtb_row, ng_row, batch, seq, tl):
    t = batch * seq
    nt = seq // tl
    nc = tl // GDN_CHUNK
    a_w = 3 * GDN_W
    row = lambda w: pl.BlockSpec((tl, w), lambda b, i: (b * nt + i, 0))
    npair = GDN_HEADS // 2
    pshape = (npair, tl, 2 * GDN_DK)
    return pl.pallas_call(
        functools.partial(_gdn_kernel, tl=tl, cpi=min(4, nc)),
        grid=(batch, nt),
        in_specs=[row(a_w), row(GDN_W), row(LANES), _const_spec(conv_w.shape),
                  _const_spec(alog_row.shape), _const_spec(dtb_row.shape), _const_spec(ng_row.shape)],
        out_specs=row(GDN_W),
        out_shape=jax.ShapeDtypeStruct((t, GDN_W), BF16),
        scratch_shapes=[
            pltpu.VMEM((tl + SUBLANES, a_w), F32),
            pltpu.VMEM((tl, LANES), F32),
            pltpu.VMEM((tl, GDN_W), F32),
            pltpu.VMEM((tl, GDN_W), F32),
            pltpu.VMEM((tl, GDN_W), F32),
            pltpu.VMEM(pshape, F32),
            pltpu.VMEM(pshape, BF16),
            pltpu.VMEM(pshape, BF16),
            pltpu.VMEM(pshape, BF16),
            pltpu.VMEM((npair, tl, 2 * GDN_CHUNK), BF16),
            pltpu.VMEM((npair * nc, SUBLANES, 2 * LANES), F32),
            pltpu.VMEM((GDN_HEADS, GDN_DK, GDN_DV), F32),
            pltpu.VMEM((tl, GDN_W), F32),
        ],
        compiler_params=pltpu.CompilerParams(dimension_semantics=("arbitrary", "arbitrary"),
                                             vmem_limit_bytes=VMEM_LIMIT),
        name="gdn",
    )(qkva, z, gates, conv_w, alog_row, dtb_row, ng_row)


def _rope(x, cos_t, sin_t):
    half = ROPE_DIMS // 2
    lane = lax.broadcasted_iota(jnp.int32, x.shape, 1)
    up = pltpu.roll(x, LANES - half, axis=1)
    dn = pltpu.roll(x, half, axis=1)
    partner = jnp.where((lane % MOBA_DH) < half, up, dn)
    return x * cos_t + partner * sin_t


def _moba_kernel(q_ref, k_ref, v_ref, cos_ref, sin_ref, o_ref, ka_s, vb_s, kmean_s,
                 qa_s, s_all, m_run, l_acc, acc_s, *, nb):
    blk = MOBA_BLOCK
    nbp = -(-nb // SUBLANES) * SUBLANES
    assert nbp <= MOBA_DH, "block-selection columns must fit beside one head's features"
    lane = lax.broadcasted_iota(jnp.int32, (blk, LANES), 1)
    head_a = lane < MOBA_DH
    sel_off = (MOBA_DH, 0)
    scale = MOBA_DH ** -0.5

    kmean_s[...] = jnp.zeros_like(kmean_s)

    def prep(n, carry):
        rows = pl.ds(pl.multiple_of(n * blk, blk), blk)
        kr = _rope(k_ref[rows, :], cos_ref[rows, :], sin_ref[rows, :])
        ka_s[0, rows, :] = jnp.where(head_a, kr, jnp.where(lane == sel_off[0] + n, 1.0, 0.0)).astype(BF16)
        ka_s[1, rows, :] = jnp.where(head_a, jnp.where(lane == sel_off[1] + n, 1.0, 0.0), kr).astype(BF16)
        vb_s[rows, :] = v_ref[rows, :].astype(BF16)
        kmean_s[pl.ds(n, 1), :] = jnp.mean(kr, axis=0, keepdims=True)
        return carry

    lax.fori_loop(0, nb, prep, 0)

    kmean = kmean_s[0:nbp, :]
    brow = lax.broadcasted_iota(jnp.int32, (nbp, blk), 0)

    def gating(it, carry):
        items = []
        for d in range(2):
            i = 2 * it + d
            qrows = pl.ds(pl.multiple_of(i * blk, blk), blk)
            q = _rope(q_ref[qrows, :], cos_ref[qrows, :], sin_ref[qrows, :])
            for h, qh in enumerate((jnp.where(head_a, q, 0.0), jnp.where(head_a, 0.0, q))):
                items.append((i, h, qrows, qh))
        gs = [jnp.where(brow < i, _dot_nt(kmean, qh, HIGHEST), -jnp.inf) for i, _, _, qh in items]
        sels = [brow == i for i, _, _, _ in items]
        for r in range(MOBA_TOPK):
            for n, (i, _, _, _) in enumerate(items):
                g = gs[n]
                m = jnp.max(g, axis=0, keepdims=True)
                first = jnp.min(jnp.where(g == m, brow, nbp), axis=0, keepdims=True)
                pick = brow == first
                sels[n] = jnp.logical_or(sels[n], jnp.logical_and(pick, r < i))
                gs[n] = jnp.where(pick, -jnp.inf, g)
        for (i, h, qrows, qh), sel in zip(items, sels):
            pieces = [jnp.where(sel, 0.0, NEG_BIG), jnp.zeros((LANES - sel_off[h] - nbp, blk), F32)]
            if sel_off[h]:
                pieces.insert(0, jnp.zeros((sel_off[h], blk), F32))
            bias = jnp.concatenate(pieces, axis=0).T
            own_half = head_a if h == 0 else jnp.logical_not(head_a)
            qa_s[h, qrows, :] = jnp.where(own_half, qh * scale, bias).astype(BF16)
        return carry

    lax.fori_loop(0, nb // 2, gating, 0)

    sb = 2 * blk
    causal = (lax.broadcasted_iota(jnp.int32, (sb, sb), 1)
              <= lax.broadcasted_iota(jnp.int32, (sb, sb), 0))
    head_a2 = lax.broadcasted_iota(jnp.int32, (sb, LANES), 1) < MOBA_DH
    ntile = sb // LANES

    def rows_of(idx):
        return pl.ds(pl.multiple_of(idx * sb, sb), sb)

    def row_max(s):
        parts = [s[:, t * LANES:(t + 1) * LANES] for t in range(ntile)]
        return jnp.maximum(jnp.maximum(parts[0], parts[1]), jnp.maximum(parts[2], parts[3]))

    def loop_by_two(n, steps):
        def body(t, c):
            steps(2 * t, 2)
            return c
        lax.fori_loop(0, n // 2, body, 0)

        @pl.when(n % 2 == 1)
        def _():
            steps(n - 1, 1)

    def qpair(qi, carry):
        qrows = rows_of(qi)

        for h in range(2):
            m_run[h] = jnp.full((sb, LANES), NEG_BIG, F32)

        def pass1(j0, count):
            dots = [(h, j0 + d, _dot_nt(qa_s[h, qrows, :], ka_s[h, rows_of(j0 + d), :]))
                    for d in range(count) for h in range(2)]
            for h, j, s in dots:
                s = s * LOG2E
                s_all[h, j] = s
                m_run[h] = jnp.maximum(m_run[h], row_max(s))

        loop_by_two(qi, pass1)
        for h in range(2):
            s = jnp.where(causal, _dot_nt(qa_s[h, qrows, :], ka_s[h, qrows, :]) * LOG2E, NEG_BIG)
            s_all[h, qi] = s
            m_run[h] = jnp.maximum(m_run[h], row_max(s))

        for h in range(2):
            m_run[h] = jnp.broadcast_to(jnp.max(m_run[h], axis=-1, keepdims=True), (sb, LANES))
            l_acc[h] = jnp.zeros((sb, LANES), F32)
            acc_s[h] = jnp.zeros((sb, LANES), F32)

        def pass2(j0, count):
            v2 = vb_s[pl.ds(pl.multiple_of(j0 * sb, sb), count * sb), :]
            for h in range(2):
                mb = m_run[h]
                ps = [jnp.exp2(s_all[h, j0 + d, :, t * LANES:(t + 1) * LANES] - mb)
                      for d in range(count) for t in range(ntile)]
                row_sum = ps[0]
                for p in ps[1:]:
                    row_sum = row_sum + p
                l_acc[h] += row_sum
                acc_s[h] += _dot(jnp.concatenate(ps, axis=1).astype(BF16), v2)

        loop_by_two(qi + 1, pass2)
        o_a = acc_s[0] / jnp.sum(l_acc[0], axis=-1, keepdims=True)
        o_b = acc_s[1] / jnp.sum(l_acc[1], axis=-1, keepdims=True)
        o_ref[qrows, :] = jnp.where(head_a2, o_a, o_b).astype(o_ref.dtype)
        return carry

    lax.fori_loop(0, nb // 2, qpair, 0)


def _moba(qkvb, cos_t, sin_t, batch, seq):
    t = batch * seq
    nb = seq // MOBA_BLOCK
    assert nb % 2 == 0, "query and key blocks are processed in pairs"
    sb = 2 * MOBA_BLOCK
    npair = MOBA_W // LANES
    col = lambda off: pl.BlockSpec((seq, LANES), lambda b, p: (b, off + p))
    return pl.pallas_call(
        functools.partial(_moba_kernel, nb=nb),
        grid=(batch, npair),
        in_specs=[col(0), col(npair), col(2 * npair), _const_spec(cos_t.shape), _const_spec(sin_t.shape)],
        out_specs=col(0),
        out_shape=jax.ShapeDtypeStruct((t, MOBA_W), BF16),
        scratch_shapes=[
            pltpu.VMEM((2, seq, LANES), BF16),
            pltpu.VMEM((seq, LANES), BF16),
            pltpu.VMEM((LANES, LANES), F32),
            pltpu.VMEM((2, seq, LANES), BF16),
            pltpu.VMEM((2, nb // 2, sb, sb), F32),
            pltpu.VMEM((2, sb, LANES), F32),
            pltpu.VMEM((2, sb, LANES), F32),
            pltpu.VMEM((2, sb, LANES), F32),
        ],
        compiler_params=pltpu.CompilerParams(dimension_semantics=("arbitrary", "arbitrary"),
                                             vmem_limit_bytes=VMEM_LIMIT),
        name="moba",
    )(qkvb, qkvb, qkvb, cos_t, sin_t)


def _mixffn_kernel(oa_ref, ob_ref, x_ref, wo_ref, g1_ref, b1_ref, wup_ref, cw_ref, cb_ref, wdn_ref,
                   g2_ref, b2_ref, o_ref, buf_g, buf_v, carry_g, carry_v, hid_s, *, tm, seq, tf):
    nf = D_FF // tf
    hist = SUBLANES
    seq_start = (pl.program_id(0) * tm) % seq == 0
    mix = _dot(oa_ref[...], wo_ref[0:GDN_W, :]) + _dot(ob_ref[...], wo_ref[GDN_W:, :])
    x1 = _layer_norm(DEEPNORM_ALPHA * x_ref[...] + mix, g1_ref[...], b1_ref[...])
    xb = x1.astype(BF16)
    for f in range(nf):
        halves = []
        for buf, carry, off in ((buf_g, carry_g, 0), (buf_v, carry_v, D_FF)):
            cols = slice(off + f * tf, off + (f + 1) * tf)
            up = _dot(xb, wup_ref[:, cols])
            buf[0:hist, :] = jnp.where(seq_start, 0.0, carry[f])
            buf[hist:hist + tm, :] = up
            carry[f] = up[tm - hist:tm, :]
            y = up * cw_ref[FFN_CONV - 1:FFN_CONV, cols]
            for j in range(FFN_CONV - 1):
                r0 = hist - (FFN_CONV - 1) + j
                y = y + buf[r0:r0 + tm, :] * cw_ref[j:j + 1, cols]
            halves.append(y + cb_ref[:, cols])
        hid_s[:, f * tf:(f + 1) * tf] = (_silu(halves[0]) * halves[1]).astype(BF16)
    y = DEEPNORM_ALPHA * x1 + _dot(hid_s[...], wdn_ref[...])
    o_ref[...] = _layer_norm(y, g2_ref[...], b2_ref[...])


def _mixffn(o_a, o_b, x2, w_out, g1, b1, w_up, conv_w, conv_b, w_down, g2, b2, seq, tm, tf):
    t = x2.shape[0]
    nf = D_FF // tf
    row = lambda w: pl.BlockSpec((tm, w), lambda i: (i, 0))
    consts = (w_out, g1, b1, w_up, conv_w, conv_b, w_down, g2, b2)
    return pl.pallas_call(
        functools.partial(_mixffn_kernel, tm=tm, seq=seq, tf=tf),
        grid=(t // tm,),
        in_specs=[row(GDN_W), row(MOBA_W), row(D_MODEL)] + [_const_spec(c.shape) for c in consts],
        out_specs=row(D_MODEL),
        out_shape=jax.ShapeDtypeStruct((t, D_MODEL), F32),
        scratch_shapes=[
            pltpu.VMEM((tm + SUBLANES, tf), F32), pltpu.VMEM((tm + SUBLANES, tf), F32),
            pltpu.VMEM((nf, SUBLANES, tf), F32), pltpu.VMEM((nf, SUBLANES, tf), F32),
            pltpu.VMEM((tm, D_FF), BF16),
        ],
        compiler_params=pltpu.CompilerParams(dimension_semantics=("arbitrary",),
                                             vmem_limit_bytes=VMEM_LIMIT),
        name="mixffn",
    )(o_a, o_b, x2, *consts)


def _rope_tables(seq):
    half = ROPE_DIMS // 2
    inv = ROPE_THETA ** (-jnp.arange(half, dtype=F32) / half)
    ang = jnp.arange(seq, dtype=jnp.int32).astype(F32)[:, None] * inv[None, :]
    cos, sin = jnp.cos(ang), jnp.sin(ang)
    rest = MOBA_DH - ROPE_DIMS
    cos_h = jnp.concatenate([cos, cos, jnp.ones((seq, rest), F32)], -1)
    sin_h = jnp.concatenate([-sin, sin, jnp.zeros((seq, rest), F32)], -1)
    reps = LANES // MOBA_DH
    return jnp.tile(cos_h, (1, reps)), jnp.tile(sin_h, (1, reps))


def _pad_lanes(v):
    return jnp.pad(v.astype(F32), (0, LANES - v.shape[0]))[None, :]


def kernel(x, w_in, gdn_conv_w, gdn_a_log, gdn_dt_bias, gdn_norm_g, w_out, ln1_g, ln1_b,
           w_up, ffn_conv_w, ffn_conv_b, w_down, ln2_g, ln2_b):
    batch, seq, d = x.shape
    t = batch * seq
    tm = min(512, seq)
    tl = min(512, seq)
    tf = 256
    cos_t, sin_t = _rope_tables(seq)
    g_off = 3 * GDN_W
    z_off = g_off + 2 * GDN_HEADS
    x2 = x.reshape(t, d)
    for l in range(DEPTH):
        w = w_in[l]
        w_main = jnp.concatenate([w[:, :g_off], w[:, z_off:]], axis=1).astype(BF16)
        w_gate = jnp.pad(w[:, g_off:z_off], ((0, 0), (0, LANES - 2 * GDN_HEADS))).astype(BF16)
        qkva, z, gates, qkvb = _inproj(x2, w_main, w_gate, tm)
        o_a = _gdn(qkva, z, gates, gdn_conv_w[l], _pad_lanes(gdn_a_log[l]), _pad_lanes(gdn_dt_bias[l]),
                   gdn_norm_g[l][None, :], batch, seq, tl)
        o_b = _moba(qkvb, cos_t, sin_t, batch, seq)
        x2 = _mixffn(o_a, o_b, x2, w_out[l].astype(BF16), ln1_g[l][None, :], ln1_b[l][None, :],
                     w_up[l].astype(BF16), ffn_conv_w[l], ffn_conv_b[l][None, :], w_down[l].astype(BF16),
                     ln2_g[l][None, :], ln2_b[l][None, :], seq, tm, tf)
    return x2.reshape(batch, seq, d)
```

```python
import functools

import jax
import jax.numpy as jnp
import numpy as np
from jax import lax
from jax.experimental import pallas as pl
from jax.experimental.pallas import tpu as pltpu

F32 = jnp.float32
BF16 = jnp.bfloat16

D_MODEL = 1024
DEPTH = 2
GDN_HEADS = 4
GDN_DK = 128
GDN_DV = 128
GDN_CONV = 4
GDN_CHUNK = 64
MOBA_HEADS = 8
MOBA_DH = 64
MOBA_BLOCK = 256
MOBA_TOPK = 3
ROPE_DIMS = MOBA_DH // 4
ROPE_THETA = 500000.0
GDN_W = GDN_HEADS * GDN_DK
MOBA_W = MOBA_HEADS * MOBA_DH
D_FF = 2816
FFN_CONV = 3
DEEPNORM_ALPHA = (2 * DEPTH) ** 0.25
LN_EPS = 1e-5
NORM_EPS = 1e-6

LANES = 128
SUBLANES = 8
NEG_BIG = -1e30
LOG2E = 1.4426950408889634
VMEM_LIMIT = 56 * 1024 * 1024


def _dot(a, b, precision=None):
    return jnp.dot(a, b, preferred_element_type=F32, precision=precision)


def _dot_nt(a, b, precision=None):
    return lax.dot_general(a, b, (((1,), (1,)), ((), ())), preferred_element_type=F32,
                           precision=precision)


def _dot_tn(a, b, precision=None):
    return lax.dot_general(a, b, (((0,), (0,)), ((), ())), preferred_element_type=F32,
                           precision=precision)


def _split2(x):
    hi = x.astype(BF16)
    return hi, (x - hi.astype(F32)).astype(BF16)


def _split3(x):
    hi = x.astype(BF16)
    r1 = x - hi.astype(F32)
    mid = r1.astype(BF16)
    return hi, mid, (r1 - mid.astype(F32)).astype(BF16)


def _dot3(a, b):
    return _dot(a[0], b[0]) + (_dot(a[0], b[1]) + _dot(a[1], b[0]))


def _sigmoid(x):
    return 1.0 / (1.0 + jnp.exp(-x))


def _silu(x):
    return x * _sigmoid(x)


def _layer_norm(y, g, b):
    mu = jnp.mean(y, axis=-1, keepdims=True)
    d = y - mu
    var = jnp.mean(d * d, axis=-1, keepdims=True)
    return d * lax.rsqrt(var + LN_EPS) * g + b


def _const_spec(shape):
    nd = len(shape)
    return pl.BlockSpec(shape, lambda *_: (0,) * nd, pipeline_mode=pl.Buffered(1))


def _inproj_kernel(x_ref, wm_ref, wg_ref, qkva_ref, z_ref, gates_ref, qkvb_ref):
    xb = x_ref[...].astype(BF16)
    a_w = 3 * GDN_W
    qkva_ref[...] = _dot(xb, wm_ref[:, 0:a_w])
    z_ref[...] = _dot(xb, wm_ref[:, a_w:a_w + GDN_W])
    qkvb_ref[...] = _dot(xb, wm_ref[:, a_w + GDN_W:])
    gates_ref[...] = _dot(xb, wg_ref[...])


def _inproj(x2, w_main, w_gate, tm):
    t = x2.shape[0]
    a_w = 3 * GDN_W
    b_w = 3 * MOBA_W
    row = lambda w: pl.BlockSpec((tm, w), lambda i: (i, 0))
    return pl.pallas_call(
        _inproj_kernel,
        grid=(t // tm,),
        in_specs=[row(D_MODEL), _const_spec(w_main.shape), _const_spec(w_gate.shape)],
        out_specs=[row(a_w), row(GDN_W), row(LANES), row(b_w)],
        out_shape=[jax.ShapeDtypeStruct((t, a_w), F32), jax.ShapeDtypeStruct((t, GDN_W), F32),
                   jax.ShapeDtypeStruct((t, LANES), F32), jax.ShapeDtypeStruct((t, b_w), F32)],
        compiler_params=pltpu.CompilerParams(dimension_semantics=("arbitrary",),
                                             vmem_limit_bytes=VMEM_LIMIT),
        name="inproj",
    )(x2, w_main, w_gate)


def _gdn_kernel(qkv_ref, z_ref, gates_ref, cw_ref, alog_ref, dtb_ref, ng_ref, o_ref,
                xbuf, gact, qs, ks, vs, u_s, w_s, qd_s, kd_s, at_s, egl_s, st_ref, o_s, *, tl, cpi):
    c = GDN_CHUNK
    nc = tl // c
    hist = SUBLANES

    @pl.when(pl.program_id(1) == 0)
    def _():
        xbuf[0:hist, :] = jnp.zeros((hist, 3 * GDN_W), F32)
        st_ref[...] = jnp.zeros_like(st_ref)

    xbuf[hist:hist + tl, :] = qkv_ref[...]

    g8 = gates_ref[...].T[0:2 * GDN_HEADS, :]
    hrow = lax.broadcasted_iota(jnp.int32, g8.shape, 0)
    sp_in = g8 + dtb_ref[:, 0:1]
    softplus = jnp.maximum(sp_in, 0.0) + jnp.log(1.0 + jnp.exp(-jnp.abs(sp_in)))
    act = jnp.where(hrow < GDN_HEADS, -jnp.exp(alog_ref[:, 0:1]) * softplus, _sigmoid(g8))
    gact[...] = jnp.concatenate([act, jnp.zeros((LANES - 2 * GDN_HEADS, tl), F32)], axis=0).T

    for part, dst in enumerate((qs, ks, vs)):
        for h in range(GDN_HEADS):
            c0 = part * GDN_W + h * GDN_DK
            acc = None
            for j in range(GDN_CONV):
                r0 = hist - (GDN_CONV - 1) + j
                term = xbuf[r0:r0 + tl, c0:c0 + GDN_DK] * cw_ref[j:j + 1, c0:c0 + GDN_DK]
                acc = term if acc is None else acc + term
            y = _silu(acc)
            if part < 2:
                y = y * lax.rsqrt(jnp.sum(y * y, axis=-1, keepdims=True) + NORM_EPS)
            if part == 0:
                y = y * (GDN_DK ** -0.5)
            dst[:, h * GDN_DK:(h + 1) * GDN_DK] = y
    xbuf[0:hist, :] = xbuf[tl:tl + hist, :]

    ii = lax.broadcasted_iota(jnp.int32, (c, LANES), 0)
    lane = lax.broadcasted_iota(jnp.int32, (c, LANES), 1)
    jj = lane % c
    left = lane < c
    mask_l = jnp.where(left, 1.0, 0.0).astype(BF16)
    mask_r = jnp.where(left, 0.0, 1.0).astype(BF16)
    eye_p = jnp.where(ii == jj, 1.0, 0.0)
    i3 = lax.broadcasted_iota(jnp.int32, (c, 3 * c), 0)
    t3 = lax.broadcasted_iota(jnp.int32, (c, 3 * c), 1) % c
    ltri3 = jnp.where(i3 >= t3, 1.0, 0.0).astype(BF16)

    def blockdiag(p):
        return jnp.concatenate([p * mask_l, p * mask_r], axis=0)

    def blockdiag_wide(x):
        n = x.shape[1] // 2
        z = jnp.zeros((c, n), BF16)
        return jnp.concatenate([jnp.concatenate([x[:, :n], z], axis=1),
                                jnp.concatenate([z, x[:, n:]], axis=1)], axis=0)

    def dot3(a, b_hi, b_lo):
        lhs = jnp.concatenate([a[0], a[1], a[0]], axis=1)
        return _dot(lhs, jnp.concatenate([b_hi, b_hi, b_lo], axis=0))

    def pair_gates(rows, p):
        gblk = gact[rows, :]
        gc_all = _dot(ltri3, jnp.concatenate(_split3(gblk), axis=0))
        bc = lambda a, l: jnp.broadcast_to(a[:, l:l + 1], (c, LANES))
        gc_cat = jnp.concatenate([bc(gc_all, 2 * p), bc(gc_all, 2 * p + 1)], axis=1)
        b_cat = jnp.concatenate([bc(gblk, GDN_HEADS + 2 * p), bc(gblk, GDN_HEADS + 2 * p + 1)], axis=1)
        return gc_cat, b_cat

    def phase1(it, carry):
        chains = [(it * cpi + cc, p) for cc in range(cpi) for p in range(GDN_HEADS // 2)]
        t_mats, pws, rhs = [], [], []
        for ci, p in chains:
            rows = pl.ds(pl.multiple_of(ci * c, c), c)
            cols = slice(p * 2 * GDN_DK, (p + 1) * 2 * GDN_DK)
            gc_cat, b_cat = pair_gates(rows, p)
            gc_p = jnp.where(left, gc_cat[:, :LANES], gc_cat[:, LANES:])
            gcr_p = jnp.sum(jnp.where(ii == jj, gc_p, 0.0), axis=0, keepdims=True)
            decay = jnp.exp(jnp.minimum(gc_p - gcr_p, 0.0))
            q = qs[rows, cols]
            k = ks[rows, cols]
            kb = k * b_cat
            kq = _dot_nt(jnp.concatenate([kb.astype(BF16), q.astype(BF16)], axis=0),
                         blockdiag_wide(k.astype(BF16)))
            a_mat = jnp.where(ii > jj, kq[:c] * decay, 0.0)
            at_s[p, rows, :] = jnp.where(ii >= jj, kq[c:] * decay, 0.0).astype(BF16)
            gl = gc_cat[c - 1:c, :]
            egc = jnp.exp(gc_cat)
            qd_s[p, rows, :] = (q * egc).astype(BF16)
            kd_s[p * nc + ci] = (k * jnp.exp(gl - gc_cat)).T.astype(BF16)
            egl_s[p * nc + ci] = jnp.broadcast_to(jnp.exp(gl), (SUBLANES, 2 * LANES))
            vb = vs[rows, cols] * b_cat
            ke = kb * egc
            x = _split2(jnp.concatenate([vb[:, :GDN_DV], ke[:, :GDN_DK], vb[:, GDN_DV:], ke[:, GDN_DK:]],
                                        axis=1))
            rhs.append((blockdiag_wide(x[0]), blockdiag_wide(x[1])))
            t_mats.append(eye_p - a_mat)
            pws.append(_split2(a_mat))
        for _ in range(5):
            pws = [_split2(dot3(pw, blockdiag(pw[0]), blockdiag(pw[1]))) for pw in pws]
            t_mats = [t + dot3(_split2(t), blockdiag(pw[0]), blockdiag(pw[1]))
                      for t, pw in zip(t_mats, pws)]
        for (ci, p), t_mat, x in zip(chains, t_mats, rhs):
            rows = pl.ds(pl.multiple_of(ci * c, c), c)
            sol = dot3(_split2(t_mat), x[0], x[1])
            u_s[p, rows, :] = jnp.concatenate([sol[:, 0:128], sol[:, 256:384]], axis=1)
            w_s[p, rows, :] = jnp.concatenate([sol[:, 128:256], sol[:, 384:512]], axis=1).astype(BF16)
        return carry

    lax.fori_loop(0, nc // cpi, phase1, 0)

    zs = jnp.zeros((GDN_DK, GDN_DV), BF16)

    def phase2(ci, carry):
        rows = pl.ds(pl.multiple_of(ci * c, c), c)
        pairs = range(GDN_HEADS // 2)
        sts = [(st_ref[2 * p], st_ref[2 * p + 1]) for p in pairs]
        rs = []
        for p in pairs:
            bds = jnp.concatenate([jnp.concatenate([sts[p][0].astype(BF16), zs], axis=1),
                                   jnp.concatenate([zs, sts[p][1].astype(BF16)], axis=1)], axis=0)
            rs.append(_dot(jnp.concatenate([w_s[p, rows, :], qd_s[p, rows, :]], axis=0), bds))
        vnbs = [(u_s[p, rows, :] - rs[p][:c]).astype(BF16) for p in pairs]
        upds = [_dot(kd_s[p * nc + ci], vnbs[p]) for p in pairs]
        for p in pairs:
            o_s[rows, p * 2 * GDN_DV:(p + 1) * 2 * GDN_DV] = (
                rs[p][c:] + _dot(at_s[p, rows, :], blockdiag_wide(vnbs[p])))
        for p in pairs:
            egl = egl_s[p * nc + ci][0:1, :]
            st_ref[2 * p] = sts[p][0] * egl[:, :GDN_DV] + upds[p][:GDN_DK, :GDN_DV]
            st_ref[2 * p + 1] = sts[p][1] * egl[:, GDN_DV:] + upds[p][GDN_DK:, GDN_DV:]
        return carry

    lax.fori_loop(0, nc, phase2, 0)

    for h in range(GDN_HEADS):
        cols = slice(h * GDN_DV, (h + 1) * GDN_DV)
        o = o_s[:, cols]
        o = o * lax.rsqrt(jnp.mean(o * o, axis=-1, keepdims=True) + NORM_EPS) * ng_ref[...]
        o_ref[:, cols] = (o * _silu(z_ref[:, cols])).astype(o_ref.dtype)


def _gdn(qkva, z, gates, conv_w, alog_row, dtb_row, ng_row, batch, seq, tl):
    t = batch * seq
    nt = seq // tl
    nc = tl // GDN_CHUNK
    a_w = 3 * GDN_W
    row = lambda w: pl.BlockSpec((tl, w), lambda b, i: (b * nt + i, 0))
    npair = GDN_HEADS // 2
    pshape = (npair, tl, 2 * GDN_DK)
    return pl.pallas_call(
        functools.partial(_gdn_kernel, tl=tl, cpi=min(4, nc)),
        grid=(batch, nt),
        in_specs=[row(a_w), row(GDN_W), row(LANES), _const_spec(conv_w.shape),
                  _const_spec(alog_row.shape), _const_spec(dtb_row.shape), _const_spec(ng_row.shape)],
        out_specs=row(GDN_W),
        out_shape=jax.ShapeDtypeStruct((t, GDN_W), BF16),
        scratch_shapes=[
            pltpu.VMEM((tl + SUBLANES, a_w), F32),
            pltpu.VMEM((tl, LANES), F32),
            pltpu.VMEM((tl, GDN_W), F32),
            pltpu.VMEM((tl, GDN_W), F32),
            pltpu.VMEM((tl, GDN_W), F32),
            pltpu.VMEM(pshape, F32),
            pltpu.VMEM(pshape, BF16),
            pltpu.VMEM(pshape, BF16),
            pltpu.VMEM((npair * nc, 2 * GDN_DK, GDN_CHUNK), BF16),
            pltpu.VMEM((npair, tl, 2 * GDN_CHUNK), BF16),
            pltpu.VMEM((npair * nc, SUBLANES, 2 * LANES), F32),
            pltpu.VMEM((GDN_HEADS, GDN_DK, GDN_DV), F32),
            pltpu.VMEM((tl, GDN_W), F32),
        ],
        compiler_params=pltpu.CompilerParams(dimension_semantics=("arbitrary", "arbitrary"),
                                             vmem_limit_bytes=VMEM_LIMIT),
        name="gdn",
    )(qkva, z, gates, conv_w, alog_row, dtb_row, ng_row)


def _rope(x, cos_t, sin_t):
    half = ROPE_DIMS // 2
    lane = lax.broadcasted_iota(jnp.int32, x.shape, 1)
    up = pltpu.roll(x, LANES - half, axis=1)
    dn = pltpu.roll(x, half, axis=1)
    partner = jnp.where((lane % MOBA_DH) < half, up, dn)
    return x * cos_t + partner * sin_t


def _moba_kernel(q_ref, k_ref, v_ref, cos_ref, sin_ref, o_ref, ka_s, vb_s, kmean_s,
                 qa_s, s_all, m_run, acc_s, *, nb):
    blk = MOBA_BLOCK
    nbp = -(-nb // SUBLANES) * SUBLANES
    assert nbp <= MOBA_DH, "block-selection columns must fit beside one head's features"
    lane = lax.broadcasted_iota(jnp.int32, (blk, LANES), 1)
    head_a = lane < MOBA_DH
    sel_off = (MOBA_DH, 0)
    scale = MOBA_DH ** -0.5

    kmean_s[...] = jnp.zeros_like(kmean_s)

    def prep(n, carry):
        rows = pl.ds(pl.multiple_of(n * blk, blk), blk)
        kr = _rope(k_ref[rows, :], cos_ref[rows, :], sin_ref[rows, :])
        ka_s[0, rows, :] = jnp.where(head_a, kr, jnp.where(lane == sel_off[0] + n, 1.0, 0.0)).astype(BF16)
        ka_s[1, rows, :] = jnp.where(head_a, jnp.where(lane == sel_off[1] + n, 1.0, 0.0), kr).astype(BF16)
        v = v_ref[rows, :]
        vb_s[0, rows, :] = jnp.where(head_a, v, 1.0).astype(BF16)
        vb_s[1, rows, :] = jnp.where(head_a, 1.0, v).astype(BF16)
        kmean_s[pl.ds(n, 1), :] = jnp.mean(kr, axis=0, keepdims=True)
        return carry

    lax.fori_loop(0, nb, prep, 0)

    km_hi, km_lo = _split2(kmean_s[0:nbp, :])
    kmean3 = jnp.concatenate([km_hi, km_lo, km_hi], axis=1)
    brow = lax.broadcasted_iota(jnp.int32, (nbp, blk), 0)

    def gating(it, carry):
        items = []
        for d in range(2):
            i = 2 * it + d
            qrows = pl.ds(pl.multiple_of(i * blk, blk), blk)
            q = _rope(q_ref[qrows, :], cos_ref[qrows, :], sin_ref[qrows, :])
            for h, qh in enumerate((jnp.where(head_a, q, 0.0), jnp.where(head_a, 0.0, q))):
                items.append((i, h, qrows, qh))
        gs = []
        for i, _, _, qh in items:
            q_hi, q_lo = _split2(qh)
            gate = _dot_nt(kmean3, jnp.concatenate([q_hi, q_hi, q_lo], axis=1))
            gs.append(jnp.where(brow < i, gate, -jnp.inf))
        sels = [brow == i for i, _, _, _ in items]
        for r in range(MOBA_TOPK):
            for n, (i, _, _, _) in enumerate(items):
                g = gs[n]
                m = jnp.max(g, axis=0, keepdims=True)
                first = jnp.min(jnp.where(g == m, brow, nbp), axis=0, keepdims=True)
                pick = brow == first
                sels[n] = jnp.logical_or(sels[n], jnp.logical_and(pick, r < i))
                gs[n] = jnp.where(pick, -jnp.inf, g)
        for (i, h, qrows, qh), sel in zip(items, sels):
            pieces = [jnp.where(sel, 0.0, NEG_BIG), jnp.zeros((LANES - sel_off[h] - nbp, blk), F32)]
            if sel_off[h]:
                pieces.insert(0, jnp.zeros((sel_off[h], blk), F32))
            bias = jnp.concatenate(pieces, axis=0).T
            own_half = head_a if h == 0 else jnp.logical_not(head_a)
            qa_s[h, qrows, :] = jnp.where(own_half, qh * scale, bias).astype(BF16)
        return carry

    lax.fori_loop(0, nb // 2, gating, 0)

    sb = 2 * blk
    causal = (lax.broadcasted_iota(jnp.int32, (sb, sb), 1)
              <= lax.broadcasted_iota(jnp.int32, (sb, sb), 0))
    head_a2 = lax.broadcasted_iota(jnp.int32, (sb, LANES), 1) < MOBA_DH
    ntile = sb // LANES

    def rows_of(idx):
        return pl.ds(pl.multiple_of(idx * sb, sb), sb)

    def row_max(s):
        parts = [s[:, t * LANES:(t + 1) * LANES] for t in range(ntile)]
        return jnp.maximum(jnp.maximum(parts[0], parts[1]), jnp.maximum(parts[2], parts[3]))

    def loop_by_two(n, steps):
        def body(t, c):
            steps(2 * t, 2)
            return c
        lax.fori_loop(0, n // 2, body, 0)

        @pl.when(n % 2 == 1)
        def _():
            steps(n - 1, 1)

    def qpair(qi, carry):
        qrows = rows_of(qi)

        for h in range(2):
            m_run[h] = jnp.full((sb, LANES), NEG_BIG, F32)

        def pass1(j0, count):
            dots = [(h, j0 + d, _dot_nt(qa_s[h, qrows, :], ka_s[h, rows_of(j0 + d), :]))
                    for d in range(count) for h in range(2)]
            for h, j, s in dots:
                s = s * LOG2E
                s_all[h, j] = s
                m_run[h] = jnp.maximum(m_run[h], row_max(s))

        loop_by_two(qi, pass1)
        for h in range(2):
            s = jnp.where(causal, _dot_nt(qa_s[h, qrows, :], ka_s[h, qrows, :]) * LOG2E, NEG_BIG)
            s_all[h, qi] = s
            m_run[h] = jnp.maximum(m_run[h], row_max(s))

        for h in range(2):
            m_run[h] = jnp.broadcast_to(jnp.max(m_run[h], axis=-1, keepdims=True), (sb, LANES))
            acc_s[h] = jnp.zeros((sb, LANES), F32)

        def pass2(j0, count):
            krows = pl.ds(pl.multiple_of(j0 * sb, sb), count * sb)
            for h in range(2):
                mb = m_run[h]
                ps = [jnp.exp2(s_all[h, j0 + d, :, t * LANES:(t + 1) * LANES] - mb)
                      for d in range(count) for t in range(ntile)]
                acc_s[h] += _dot(jnp.concatenate(ps, axis=1).astype(BF16), vb_s[h, krows, :])

        loop_by_two(qi + 1, pass2)
        acc_a, acc_b = acc_s[0], acc_s[1]
        o_a = acc_a / jnp.max(jnp.where(head_a2, -jnp.inf, acc_a), axis=-1, keepdims=True)
        o_b = acc_b / jnp.max(jnp.where(head_a2, acc_b, -jnp.inf), axis=-1, keepdims=True)
        o_ref[qrows, :] = jnp.where(head_a2, o_a, o_b).astype(o_ref.dtype)
        return carry

    lax.fori_loop(0, nb // 2, qpair, 0)


def _moba(qkvb, cos_t, sin_t, batch, seq):
    t = batch * seq
    nb = seq // MOBA_BLOCK
    assert nb % 2 == 0, "query and key blocks are processed in pairs"
    sb = 2 * MOBA_BLOCK
    npair = MOBA_W // LANES
    col = lambda off: pl.BlockSpec((seq, LANES), lambda b, p: (b, off + p))
    return pl.pallas_call(
        functools.partial(_moba_kernel, nb=nb),
        grid=(batch, npair),
        in_specs=[col(0), col(npair), col(2 * npair), _const_spec(cos_t.shape), _const_spec(sin_t.shape)],
        out_specs=col(0),
        out_shape=jax.ShapeDtypeStruct((t, MOBA_W), BF16),
        scratch_shapes=[
            pltpu.VMEM((2, seq, LANES), BF16),
            pltpu.VMEM((2, seq, LANES), BF16),
            pltpu.VMEM((LANES, LANES), F32),
            pltpu.VMEM((2, seq, LANES), BF16),
            pltpu.VMEM((2, nb // 2, sb, sb), F32),
            pltpu.VMEM((2, sb, LANES), F32),
            pltpu.VMEM((2, sb, LANES), F32),
        ],
        compiler_params=pltpu.CompilerParams(dimension_semantics=("arbitrary", "arbitrary"),
                                             vmem_limit_bytes=VMEM_LIMIT),
        name="moba",
    )(qkvb, qkvb, qkvb, cos_t, sin_t)


def _mixffn_kernel(oa_ref, ob_ref, x_ref, wo_ref, g1_ref, b1_ref, wup_ref, cw_ref, cb_ref, wdn_ref,
                   g2_ref, b2_ref, o_ref, buf_g, buf_v, carry_g, carry_v, hid_s, *, tm, seq, tf):
    nf = D_FF // tf
    hist = SUBLANES
    seq_start = (pl.program_id(0) * tm) % seq == 0
    mix = _dot(oa_ref[...], wo_ref[0:GDN_W, :]) + _dot(ob_ref[...], wo_ref[GDN_W:, :])
    x1 = _layer_norm(DEEPNORM_ALPHA * x_ref[...] + mix, g1_ref[...], b1_ref[...])
    xb = x1.astype(BF16)
    for f in range(nf):
        halves = []
        for buf, carry, off in ((buf_g, carry_g, 0), (buf_v, carry_v, D_FF)):
            cols = slice(off + f * tf, off + (f + 1) * tf)
            up = _dot(xb, wup_ref[:, cols])
            buf[0:hist, :] = jnp.where(seq_start, 0.0, carry[f])
            buf[hist:hist + tm, :] = up
            carry[f] = up[tm - hist:tm, :]
            y = up * cw_ref[FFN_CONV - 1:FFN_CONV, cols]
            for j in range(FFN_CONV - 1):
                r0 = hist - (FFN_CONV - 1) + j
                y = y + buf[r0:r0 + tm, :] * cw_ref[j:j + 1, cols]
            halves.append(y + cb_ref[:, cols])
        hid_s[:, f * tf:(f + 1) * tf] = (_silu(halves[0]) * halves[1]).astype(BF16)
    y = DEEPNORM_ALPHA * x1 + _dot(hid_s[...], wdn_ref[...])
    o_ref[...] = _layer_norm(y, g2_ref[...], b2_ref[...])


def _mixffn(o_a, o_b, x2, w_out, g1, b1, w_up, conv_w, conv_b, w_down, g2, b2, seq, tm, tf):
    t = x2.shape[0]
    nf = D_FF // tf
    row = lambda w: pl.BlockSpec((tm, w), lambda i: (i, 0))
    consts = (w_out, g1, b1, w_up, conv_w, conv_b, w_down, g2, b2)
    return pl.pallas_call(
        functools.partial(_mixffn_kernel, tm=tm, seq=seq, tf=tf),
        grid=(t // tm,),
        in_specs=[row(GDN_W), row(MOBA_W), row(D_MODEL)] + [_const_spec(c.shape) for c in consts],
        out_specs=row(D_MODEL),
        out_shape=jax.ShapeDtypeStruct((t, D_MODEL), F32),
        scratch_shapes=[
            pltpu.VMEM((tm + SUBLANES, tf), F32), pltpu.VMEM((tm + SUBLANES, tf), F32),
            pltpu.VMEM((nf, SUBLANES, tf), F32), pltpu.VMEM((nf, SUBLANES, tf), F32),
            pltpu.VMEM((tm, D_FF), BF16),
        ],
        compiler_params=pltpu.CompilerParams(dimension_semantics=("arbitrary",),
                                             vmem_limit_bytes=VMEM_LIMIT),
        name="mixffn",
    )(o_a, o_b, x2, *consts)


def _rope_tables(seq):
    half = ROPE_DIMS // 2
    inv = ROPE_THETA ** (-jnp.arange(half, dtype=F32) / half)
    ang = jnp.arange(seq, dtype=jnp.int32).astype(F32)[:, None] * inv[None, :]
    cos, sin = jnp.cos(ang), jnp.sin(ang)
    rest = MOBA_DH - ROPE_DIMS
    cos_h = jnp.concatenate([cos, cos, jnp.ones((seq, rest), F32)], -1)
    sin_h = jnp.concatenate([-sin, sin, jnp.zeros((seq, rest), F32)], -1)
    reps = LANES // MOBA_DH
    return jnp.tile(cos_h, (1, reps)), jnp.tile(sin_h, (1, reps))


def _head_rows(v):
    col = jnp.pad(v.astype(F32), (0, SUBLANES - v.shape[0]))
    return jnp.broadcast_to(col[:, None], (SUBLANES, LANES))


def kernel(x, w_in, gdn_conv_w, gdn_a_log, gdn_dt_bias, gdn_norm_g, w_out, ln1_g, ln1_b,
           w_up, ffn_conv_w, ffn_conv_b, w_down, ln2_g, ln2_b):
    batch, seq, d = x.shape
    t = batch * seq
    tm = min(512, seq)
    tl = min(512, seq)
    tf = 256
    cos_t, sin_t = _rope_tables(seq)
    g_off = 3 * GDN_W
    z_off = g_off + 2 * GDN_HEADS
    x2 = x.reshape(t, d)
    for l in range(DEPTH):
        w = w_in[l]
        w_main = jnp.concatenate([w[:, :g_off], w[:, z_off:]], axis=1).astype(BF16)
        w_gate = jnp.pad(w[:, g_off:z_off], ((0, 0), (0, LANES - 2 * GDN_HEADS))).astype(BF16)
        qkva, z, gates, qkvb = _inproj(x2, w_main, w_gate, tm)
        o_a = _gdn(qkva, z, gates, gdn_conv_w[l], _head_rows(gdn_a_log[l]), _head_rows(gdn_dt_bias[l]),
                   gdn_norm_g[l][None, :], batch, seq, tl)
        o_b = _moba(qkvb, cos_t, sin_t, batch, seq)
        x2 = _mixffn(o_a, o_b, x2, w_out[l].astype(BF16), ln1_g[l][None, :], ln1_b[l][None, :],
                     w_up[l].astype(BF16), ffn_conv_w[l], ffn_conv_b[l][None, :], w_down[l].astype(BF16),
                     ln2_g[l][None, :], ln2_b[l][None, :], seq, tm, tf)
    return x2.reshape(batch, seq, d)
```

```python
import functools

import jax
import jax.numpy as jnp
import numpy as np
from jax import lax
from jax.experimental import pallas as pl
from jax.experimental.pallas import tpu as pltpu

F32 = jnp.float32
BF16 = jnp.bfloat16

D_MODEL = 1024
DEPTH = 2
GDN_HEADS = 4
GDN_DK = 128
GDN_DV = 128
GDN_CONV = 4
GDN_CHUNK = 64
MOBA_HEADS = 8
MOBA_DH = 64
MOBA_BLOCK = 256
MOBA_TOPK = 3
ROPE_DIMS = MOBA_DH // 4
ROPE_THETA = 500000.0
GDN_W = GDN_HEADS * GDN_DK
MOBA_W = MOBA_HEADS * MOBA_DH
D_FF = 2816
FFN_CONV = 3
DEEPNORM_ALPHA = (2 * DEPTH) ** 0.25
LN_EPS = 1e-5
NORM_EPS = 1e-6

LANES = 128
SUBLANES = 8
NEG_BIG = -1e30
LOG2E = 1.4426950408889634
VMEM_LIMIT = 56 * 1024 * 1024


def _dot(a, b, precision=None):
    return jnp.dot(a, b, preferred_element_type=F32, precision=precision)


def _dot_nt(a, b, precision=None):
    return lax.dot_general(a, b, (((1,), (1,)), ((), ())), preferred_element_type=F32,
                           precision=precision)


def _dot_tn(a, b, precision=None):
    return lax.dot_general(a, b, (((0,), (0,)), ((), ())), preferred_element_type=F32,
                           precision=precision)


def _split2(x):
    hi = x.astype(BF16)
    return hi, (x - hi.astype(F32)).astype(BF16)


def _split3(x):
    hi = x.astype(BF16)
    r1 = x - hi.astype(F32)
    mid = r1.astype(BF16)
    return hi, mid, (r1 - mid.astype(F32)).astype(BF16)


def _dot3(a, b):
    return _dot(a[0], b[0]) + (_dot(a[0], b[1]) + _dot(a[1], b[0]))


def _sigmoid(x):
    return 1.0 / (1.0 + jnp.exp(-x))


def _silu(x):
    return x * _sigmoid(x)


def _layer_norm(y, g, b):
    mu = jnp.mean(y, axis=-1, keepdims=True)
    d = y - mu
    var = jnp.mean(d * d, axis=-1, keepdims=True)
    return d * lax.rsqrt(var + LN_EPS) * g + b


def _const_spec(shape):
    nd = len(shape)
    return pl.BlockSpec(shape, lambda *_: (0,) * nd, pipeline_mode=pl.Buffered(1))


def _inproj_kernel(x_ref, wm_ref, wg_ref, qkva_ref, z_ref, gates_ref, qkvb_ref):
    xb = x_ref[...].astype(BF16)
    a_w = 3 * GDN_W
    qkva_ref[...] = _dot(xb, wm_ref[:, 0:a_w])
    z_ref[...] = _dot(xb, wm_ref[:, a_w:a_w + GDN_W])
    qkvb_ref[...] = _dot(xb, wm_ref[:, a_w + GDN_W:])
    gates_ref[...] = _dot(xb, wg_ref[...])


def _inproj(x2, w_main, w_gate, tm):
    t = x2.shape[0]
    a_w = 3 * GDN_W
    b_w = 3 * MOBA_W
    row = lambda w: pl.BlockSpec((tm, w), lambda i: (i, 0))
    return pl.pallas_call(
        _inproj_kernel,
        grid=(t // tm,),
        in_specs=[row(D_MODEL), _const_spec(w_main.shape), _const_spec(w_gate.shape)],
        out_specs=[row(a_w), row(GDN_W), row(LANES), row(b_w)],
        out_shape=[jax.ShapeDtypeStruct((t, a_w), F32), jax.ShapeDtypeStruct((t, GDN_W), F32),
                   jax.ShapeDtypeStruct((t, LANES), F32), jax.ShapeDtypeStruct((t, b_w), F32)],
        compiler_params=pltpu.CompilerParams(dimension_semantics=("arbitrary",),
                                             vmem_limit_bytes=VMEM_LIMIT),
        name="inproj",
    )(x2, w_main, w_gate)


def _gdn_kernel(qkv_ref, z_ref, gates_ref, cw_ref, alog_ref, dtb_ref, ng_ref, o_ref,
                xbuf, gact, qs, ks, vs, u_s, w_s, qd_s, kd_s, at_s, egl_s, st_ref, o_s, *, tl, cpi):
    c = GDN_CHUNK
    nc = tl // c
    hist = SUBLANES

    @pl.when(pl.program_id(1) == 0)
    def _():
        xbuf[0:hist, :] = jnp.zeros((hist, 3 * GDN_W), F32)
        st_ref[...] = jnp.zeros_like(st_ref)

    xbuf[hist:hist + tl, :] = qkv_ref[...]

    g8 = gates_ref[...].T[0:2 * GDN_HEADS, :]
    hrow = lax.broadcasted_iota(jnp.int32, g8.shape, 0)
    sp_in = g8 + dtb_ref[:, 0:1]
    softplus = jnp.maximum(sp_in, 0.0) + jnp.log(1.0 + jnp.exp(-jnp.abs(sp_in)))
    act = jnp.where(hrow < GDN_HEADS, -jnp.exp(alog_ref[:, 0:1]) * softplus, _sigmoid(g8))
    gact[...] = jnp.concatenate([act, jnp.zeros((LANES - 2 * GDN_HEADS, tl), F32)], axis=0).T

    for part, dst in enumerate((qs, ks, vs)):
        for h in range(GDN_HEADS):
            c0 = part * GDN_W + h * GDN_DK
            acc = None
            for j in range(GDN_CONV):
                r0 = hist - (GDN_CONV - 1) + j
                term = xbuf[r0:r0 + tl, c0:c0 + GDN_DK] * cw_ref[j:j + 1, c0:c0 + GDN_DK]
                acc = term if acc is None else acc + term
            y = _silu(acc)
            if part < 2:
                y = y * lax.rsqrt(jnp.sum(y * y, axis=-1, keepdims=True) + NORM_EPS)
            if part == 0:
                y = y * (GDN_DK ** -0.5)
            dst[:, h * GDN_DK:(h + 1) * GDN_DK] = y
    xbuf[0:hist, :] = xbuf[tl:tl + hist, :]

    ii = lax.broadcasted_iota(jnp.int32, (c, LANES), 0)
    lane = lax.broadcasted_iota(jnp.int32, (c, LANES), 1)
    jj = lane % c
    left = lane < c
    mask_l = jnp.where(left, 1.0, 0.0).astype(BF16)
    mask_r = jnp.where(left, 0.0, 1.0).astype(BF16)
    eye_p = jnp.where(ii == jj, 1.0, 0.0)
    i3 = lax.broadcasted_iota(jnp.int32, (c, 3 * c), 0)
    t3 = lax.broadcasted_iota(jnp.int32, (c, 3 * c), 1) % c
    ltri3 = jnp.where(i3 >= t3, 1.0, 0.0).astype(BF16)

    def blockdiag(p):
        return jnp.concatenate([p * mask_l, p * mask_r], axis=0)

    def blockdiag_wide(x):
        n = x.shape[1] // 2
        z = jnp.zeros((c, n), BF16)
        return jnp.concatenate([jnp.concatenate([x[:, :n], z], axis=1),
                                jnp.concatenate([z, x[:, n:]], axis=1)], axis=0)

    def dot3(a, b_hi, b_lo):
        lhs = jnp.concatenate([a[0], a[1], a[0]], axis=1)
        return _dot(lhs, jnp.concatenate([b_hi, b_hi, b_lo], axis=0))

    def pair_gates(rows, p):
        gblk = gact[rows, :]
        gc_all = _dot(ltri3, jnp.concatenate(_split3(gblk), axis=0))
        bc = lambda a, l: jnp.broadcast_to(a[:, l:l + 1], (c, LANES))
        gc_cat = jnp.concatenate([bc(gc_all, 2 * p), bc(gc_all, 2 * p + 1)], axis=1)
        b_cat = jnp.concatenate([bc(gblk, GDN_HEADS + 2 * p), bc(gblk, GDN_HEADS + 2 * p + 1)], axis=1)
        return gc_cat, b_cat

    def rows_at(ci):
        return pl.ds(ci * c, c)

    def phase1(it):
        chains = [(it * cpi + cc, p) for cc in range(cpi) for p in range(GDN_HEADS // 2)]
        t_mats, pws, rhs = [], [], []
        for ci, p in chains:
            rows = rows_at(ci)
            cols = slice(p * 2 * GDN_DK, (p + 1) * 2 * GDN_DK)
            gc_cat, b_cat = pair_gates(rows, p)
            gc_p = jnp.where(left, gc_cat[:, :LANES], gc_cat[:, LANES:])
            gcr_p = jnp.sum(jnp.where(ii == jj, gc_p, 0.0), axis=0, keepdims=True)
            decay = jnp.exp(jnp.minimum(gc_p - gcr_p, 0.0))
            q = qs[rows, cols]
            k = ks[rows, cols]
            kb = k * b_cat
            kq = _dot_nt(jnp.concatenate([kb.astype(BF16), q.astype(BF16)], axis=0),
                         blockdiag_wide(k.astype(BF16)))
            a_mat = jnp.where(ii > jj, kq[:c] * decay, 0.0)
            at_s[p, rows, :] = jnp.where(ii >= jj, kq[c:] * decay, 0.0).astype(BF16)
            gl = gc_cat[c - 1:c, :]
            egc = jnp.exp(gc_cat)
            qd_s[p, rows, :] = (q * egc).astype(BF16)
            kd_s[p * nc + ci] = (k * jnp.exp(gl - gc_cat)).T.astype(BF16)
            egl_s[p * nc + ci] = jnp.broadcast_to(jnp.exp(gl), (SUBLANES, 2 * LANES))
            vb = vs[rows, cols] * b_cat
            ke = kb * egc
            x = _split2(jnp.concatenate([vb[:, :GDN_DV], ke[:, :GDN_DK], vb[:, GDN_DV:], ke[:, GDN_DK:]],
                                        axis=1))
            rhs.append((blockdiag_wide(x[0]), blockdiag_wide(x[1])))
            t_mats.append(eye_p - a_mat)
            pws.append(_split2(a_mat))
        yield
        for _ in range(5):
            pws = [_split2(dot3(pw, blockdiag(pw[0]), blockdiag(pw[1]))) for pw in pws]
            t_mats = [t + dot3(_split2(t), blockdiag(pw[0]), blockdiag(pw[1]))
                      for t, pw in zip(t_mats, pws)]
            yield
        for (ci, p), t_mat, x in zip(chains, t_mats, rhs):
            rows = rows_at(ci)
            sol = dot3(_split2(t_mat), x[0], x[1])
            u_s[p, rows, :] = jnp.concatenate([sol[:, 0:128], sol[:, 256:384]], axis=1)
            w_s[p, rows, :] = jnp.concatenate([sol[:, 128:256], sol[:, 384:512]], axis=1).astype(BF16)

    zs = jnp.zeros((GDN_DK, GDN_DV), BF16)

    def phase2(ci):
        rows = rows_at(ci)
        pairs = range(GDN_HEADS // 2)
        sts = [(st_ref[2 * p], st_ref[2 * p + 1]) for p in pairs]
        rs = []
        for p in pairs:
            bds = jnp.concatenate([jnp.concatenate([sts[p][0].astype(BF16), zs], axis=1),
                                   jnp.concatenate([zs, sts[p][1].astype(BF16)], axis=1)], axis=0)
            rs.append(_dot(jnp.concatenate([w_s[p, rows, :], qd_s[p, rows, :]], axis=0), bds))
        vnbs = [(u_s[p, rows, :] - rs[p][:c]).astype(BF16) for p in pairs]
        upds = [_dot(kd_s[p * nc + ci], vnbs[p]) for p in pairs]
        for p in pairs:
            o_s[rows, p * 2 * GDN_DV:(p + 1) * 2 * GDN_DV] = (
                rs[p][c:] + _dot(at_s[p, rows, :], blockdiag_wide(vnbs[p])))
        for p in pairs:
            egl = egl_s[p * nc + ci][0:1, :]
            st_ref[2 * p] = sts[p][0] * egl[:, :GDN_DV] + upds[p][:GDN_DK, :GDN_DV]
            st_ref[2 * p + 1] = sts[p][1] * egl[:, GDN_DV:] + upds[p][GDN_DK:, GDN_DV:]

    ready = []
    for it in range(nc // cpi):
        for _ in phase1(it):
            if ready:
                phase2(ready.pop(0))
        ready.extend(range(it * cpi, (it + 1) * cpi))
    for ci in ready:
        phase2(ci)

    for h in range(GDN_HEADS):
        cols = slice(h * GDN_DV, (h + 1) * GDN_DV)
        o = o_s[:, cols]
        o = o * lax.rsqrt(jnp.mean(o * o, axis=-1, keepdims=True) + NORM_EPS) * ng_ref[...]
        o_ref[:, cols] = (o * _silu(z_ref[:, cols])).astype(o_ref.dtype)


def _gdn(qkva, z, gates, conv_w, alog_row, dtb_row, ng_row, batch, seq, tl):
    t = batch * seq
    nt = seq // tl
    nc = tl // GDN_CHUNK
    a_w = 3 * GDN_W
    row = lambda w: pl.BlockSpec((tl, w), lambda b, i: (b * nt + i, 0))
    npair = GDN_HEADS // 2
    pshape = (npair, tl, 2 * GDN_DK)
    return pl.pallas_call(
        functools.partial(_gdn_kernel, tl=tl, cpi=min(4, nc)),
        grid=(batch, nt),
        in_specs=[row(a_w), row(GDN_W), row(LANES), _const_spec(conv_w.shape),
                  _const_spec(alog_row.shape), _const_spec(dtb_row.shape), _const_spec(ng_row.shape)],
        out_specs=row(GDN_W),
        out_shape=jax.ShapeDtypeStruct((t, GDN_W), BF16),
        scratch_shapes=[
            pltpu.VMEM((tl + SUBLANES, a_w), F32),
            pltpu.VMEM((tl, LANES), F32),
            pltpu.VMEM((tl, GDN_W), F32),
            pltpu.VMEM((tl, GDN_W), F32),
            pltpu.VMEM((tl, GDN_W), F32),
            pltpu.VMEM(pshape, F32),
            pltpu.VMEM(pshape, BF16),
            pltpu.VMEM(pshape, BF16),
            pltpu.VMEM((npair * nc, 2 * GDN_DK, GDN_CHUNK), BF16),
            pltpu.VMEM((npair, tl, 2 * GDN_CHUNK), BF16),
            pltpu.VMEM((npair * nc, SUBLANES, 2 * LANES), F32),
            pltpu.VMEM((GDN_HEADS, GDN_DK, GDN_DV), F32),
            pltpu.VMEM((tl, GDN_W), F32),
        ],
        compiler_params=pltpu.CompilerParams(dimension_semantics=("arbitrary", "arbitrary"),
                                             vmem_limit_bytes=VMEM_LIMIT),
        name="gdn",
    )(qkva, z, gates, conv_w, alog_row, dtb_row, ng_row)


def _rope(x, cos_t, sin_t):
    half = ROPE_DIMS // 2
    lane = lax.broadcasted_iota(jnp.int32, x.shape, 1)
    up = pltpu.roll(x, LANES - half, axis=1)
    dn = pltpu.roll(x, half, axis=1)
    partner = jnp.where((lane % MOBA_DH) < half, up, dn)
    return x * cos_t + partner * sin_t


def _moba_kernel(q_ref, k_ref, v_ref, cos_ref, sin_ref, o_ref, ka_s, vb_s, kmean_s,
                 qa_s, s_all, m_run, acc_s, *, nb):
    blk = MOBA_BLOCK
    nbp = -(-nb // SUBLANES) * SUBLANES
    assert nbp <= MOBA_DH, "block-selection columns must fit beside one head's features"
    lane = lax.broadcasted_iota(jnp.int32, (blk, LANES), 1)
    head_a = lane < MOBA_DH
    sel_off = (MOBA_DH, 0)
    scale = MOBA_DH ** -0.5

    kmean_s[...] = jnp.zeros_like(kmean_s)

    def prep(n, carry):
        rows = pl.ds(pl.multiple_of(n * blk, blk), blk)
        kr = _rope(k_ref[rows, :], cos_ref[rows, :], sin_ref[rows, :])
        ka_s[0, rows, :] = jnp.where(head_a, kr, jnp.where(lane == sel_off[0] + n, 1.0, 0.0)).astype(BF16)
        ka_s[1, rows, :] = jnp.where(head_a, jnp.where(lane == sel_off[1] + n, 1.0, 0.0), kr).astype(BF16)
        v = v_ref[rows, :]
        vb_s[0, rows, :] = jnp.where(head_a, v, 1.0).astype(BF16)
        vb_s[1, rows, :] = jnp.where(head_a, 1.0, v).astype(BF16)
        kmean_s[pl.ds(n, 1), :] = jnp.mean(kr, axis=0, keepdims=True)
        return carry

    lax.fori_loop(0, nb, prep, 0)

    km_hi, km_lo = _split2(kmean_s[0:nbp, :])
    kmean3 = jnp.concatenate([km_hi, km_lo, km_hi], axis=1)
    brow = lax.broadcasted_iota(jnp.int32, (nbp, blk), 0)

    def gating(it, carry):
        items = []
        for d in range(2):
            i = 2 * it + d
            qrows = pl.ds(pl.multiple_of(i * blk, blk), blk)
            q = _rope(q_ref[qrows, :], cos_ref[qrows, :], sin_ref[qrows, :])
            for h, qh in enumerate((jnp.where(head_a, q, 0.0), jnp.where(head_a, 0.0, q))):
                items.append((i, h, qrows, qh))
        gs = []
        for i, _, _, qh in items:
            q_hi, q_lo = _split2(qh)
            gate = _dot_nt(kmean3, jnp.concatenate([q_hi, q_hi, q_lo], axis=1))
            gs.append(jnp.where(brow < i, gate, -jnp.inf))
        sels = [brow == i for i, _, _, _ in items]
        for r in range(MOBA_TOPK):
            for n, (i, _, _, _) in enumerate(items):
                g = gs[n]
                m = jnp.max(g, axis=0, keepdims=True)
                first = jnp.min(jnp.where(g == m, brow, nbp), axis=0, keepdims=True)
                pick = brow == first
                sels[n] = jnp.logical_or(sels[n], jnp.logical_and(pick, r < i))
                gs[n] = jnp.where(pick, -jnp.inf, g)
        for (i, h, qrows, qh), sel in zip(items, sels):
            pieces = [jnp.where(sel, 0.0, NEG_BIG), jnp.zeros((LANES - sel_off[h] - nbp, blk), F32)]
            if sel_off[h]:
                pieces.insert(0, jnp.zeros((sel_off[h], blk), F32))
            bias = jnp.concatenate(pieces, axis=0).T
            own_half = head_a if h == 0 else jnp.logical_not(head_a)
            qa_s[h, qrows, :] = jnp.where(own_half, qh * scale, bias).astype(BF16)
        return carry

    lax.fori_loop(0, nb // 2, gating, 0)

    sb = 2 * blk
    causal = (lax.broadcasted_iota(jnp.int32, (sb, sb), 1)
              <= lax.broadcasted_iota(jnp.int32, (sb, sb), 0))
    head_a2 = lax.broadcasted_iota(jnp.int32, (sb, LANES), 1) < MOBA_DH
    ntile = sb // LANES

    def rows_of(idx):
        return pl.ds(pl.multiple_of(idx * sb, sb), sb)

    def row_max(s):
        parts = [s[:, t * LANES:(t + 1) * LANES] for t in range(ntile)]
        return jnp.maximum(jnp.maximum(parts[0], parts[1]), jnp.maximum(parts[2], parts[3]))

    def loop_by_two(n, steps):
        def body(t, c):
            steps(2 * t, 2)
            return c
        lax.fori_loop(0, n // 2, body, 0)

        @pl.when(n % 2 == 1)
        def _():
            steps(n - 1, 1)

    def qpair(qi, carry):
        qrows = rows_of(qi)

        for h in range(2):
            m_run[h] = jnp.full((sb, LANES), NEG_BIG, F32)

        def pass1(j0, count):
            dots = [(h, j0 + d, _dot_nt(qa_s[h, qrows, :], ka_s[h, rows_of(j0 + d), :]))
                    for d in range(count) for h in range(2)]
            for h, j, s in dots:
                s = s * LOG2E
                s_all[h, j] = s
                m_run[h] = jnp.maximum(m_run[h], row_max(s))

        loop_by_two(qi, pass1)
        for h in range(2):
            s = jnp.where(causal, _dot_nt(qa_s[h, qrows, :], ka_s[h, qrows, :]) * LOG2E, NEG_BIG)
            s_all[h, qi] = s
            m_run[h] = jnp.maximum(m_run[h], row_max(s))

        for h in range(2):
            m_run[h] = jnp.broadcast_to(jnp.max(m_run[h], axis=-1, keepdims=True), (sb, LANES))
            acc_s[h] = jnp.zeros((sb, LANES), F32)

        def pass2(j0, count):
            krows = pl.ds(pl.multiple_of(j0 * sb, sb), count * sb)
            for h in range(2):
                mb = m_run[h]
                ps = [jnp.exp2(s_all[h, j0 + d, :, t * LANES:(t + 1) * LANES] - mb)
                      for d in range(count) for t in range(ntile)]
                acc_s[h] += _dot(jnp.concatenate(ps, axis=1).astype(BF16), vb_s[h, krows, :])

        loop_by_two(qi + 1, pass2)
        acc_a, acc_b = acc_s[0], acc_s[1]
        o_a = acc_a / jnp.max(jnp.where(head_a2, -jnp.inf, acc_a), axis=-1, keepdims=True)
        o_b = acc_b / jnp.max(jnp.where(head_a2, acc_b, -jnp.inf), axis=-1, keepdims=True)
        o_ref[qrows, :] = jnp.where(head_a2, o_a, o_b).astype(o_ref.dtype)
        return carry

    lax.fori_loop(0, nb // 2, qpair, 0)


def _moba(qkvb, cos_t, sin_t, batch, seq):
    t = batch * seq
    nb = seq // MOBA_BLOCK
    assert nb % 2 == 0, "query and key blocks are processed in pairs"
    sb = 2 * MOBA_BLOCK
    npair = MOBA_W // LANES
    col = lambda off: pl.BlockSpec((seq, LANES), lambda b, p: (b, off + p))
    return pl.pallas_call(
        functools.partial(_moba_kernel, nb=nb),
        grid=(batch, npair),
        in_specs=[col(0), col(npair), col(2 * npair), _const_spec(cos_t.shape), _const_spec(sin_t.shape)],
        out_specs=col(0),
        out_shape=jax.ShapeDtypeStruct((t, MOBA_W), BF16),
        scratch_shapes=[
            pltpu.VMEM((2, seq, LANES), BF16),
            pltpu.VMEM((2, seq, LANES), BF16),
            pltpu.VMEM((LANES, LANES), F32),
            pltpu.VMEM((2, seq, LANES), BF16),
            pltpu.VMEM((2, nb // 2, sb, sb), F32),
            pltpu.VMEM((2, sb, LANES), F32),
            pltpu.VMEM((2, sb, LANES), F32),
        ],
        compiler_params=pltpu.CompilerParams(dimension_semantics=("arbitrary", "arbitrary"),
                                             vmem_limit_bytes=VMEM_LIMIT),
        name="moba",
    )(qkvb, qkvb, qkvb, cos_t, sin_t)


def _mixffn_kernel(oa_ref, ob_ref, x_ref, wo_ref, g1_ref, b1_ref, wup_ref, cw_ref, cb_ref, wdn_ref,
                   g2_ref, b2_ref, o_ref, buf_g, buf_v, carry_g, carry_v, hid_s, *, tm, seq, tf):
    nf = D_FF // tf
    hist = SUBLANES
    seq_start = (pl.program_id(0) * tm) % seq == 0
    mix = _dot(oa_ref[...], wo_ref[0:GDN_W, :]) + _dot(ob_ref[...], wo_ref[GDN_W:, :])
    x1 = _layer_norm(DEEPNORM_ALPHA * x_ref[...] + mix, g1_ref[...], b1_ref[...])
    xb = x1.astype(BF16)
    for f in range(nf):
        halves = []
        for buf, carry, off in ((buf_g, carry_g, 0), (buf_v, carry_v, D_FF)):
            cols = slice(off + f * tf, off + (f + 1) * tf)
            up = _dot(xb, wup_ref[:, cols])
            buf[0:hist, :] = jnp.where(seq_start, 0.0, carry[f])
            buf[hist:hist + tm, :] = up
            carry[f] = up[tm - hist:tm, :]
            y = up * cw_ref[FFN_CONV - 1:FFN_CONV, cols]
            for j in range(FFN_CONV - 1):
                r0 = hist - (FFN_CONV - 1) + j
                y = y + buf[r0:r0 + tm, :] * cw_ref[j:j + 1, cols]
            halves.append(y + cb_ref[:, cols])
        hid_s[:, f * tf:(f + 1) * tf] = (_silu(halves[0]) * halves[1]).astype(BF16)
    y = DEEPNORM_ALPHA * x1 + _dot(hid_s[...], wdn_ref[...])
    o_ref[...] = _layer_norm(y, g2_ref[...], b2_ref[...])


def _mixffn(o_a, o_b, x2, w_out, g1, b1, w_up, conv_w, conv_b, w_down, g2, b2, seq, tm, tf):
    t = x2.shape[0]
    nf = D_FF // tf
    row = lambda w: pl.BlockSpec((tm, w), lambda i: (i, 0))
    consts = (w_out, g1, b1, w_up, conv_w, conv_b, w_down, g2, b2)
    return pl.pallas_call(
        functools.partial(_mixffn_kernel, tm=tm, seq=seq, tf=tf),
        grid=(t // tm,),
        in_specs=[row(GDN_W), row(MOBA_W), row(D_MODEL)] + [_const_spec(c.shape) for c in consts],
        out_specs=row(D_MODEL),
        out_shape=jax.ShapeDtypeStruct((t, D_MODEL), F32),
        scratch_shapes=[
            pltpu.VMEM((tm + SUBLANES, tf), F32), pltpu.VMEM((tm + SUBLANES, tf), F32),
            pltpu.VMEM((nf, SUBLANES, tf), F32), pltpu.VMEM((nf, SUBLANES, tf), F32),
            pltpu.VMEM((tm, D_FF), BF16),
        ],
        compiler_params=pltpu.CompilerParams(dimension_semantics=("arbitrary",),
                                             vmem_limit_bytes=VMEM_LIMIT),
        name="mixffn",
    )(o_a, o_b, x2, *consts)


def _rope_tables(seq):
    half = ROPE_DIMS // 2
    inv = ROPE_THETA ** (-jnp.arange(half, dtype=F32) / half)
    ang = jnp.arange(seq, dtype=jnp.int32).astype(F32)[:, None] * inv[None, :]
    cos, sin = jnp.cos(ang), jnp.sin(ang)
    rest = MOBA_DH - ROPE_DIMS
    cos_h = jnp.concatenate([cos, cos, jnp.ones((seq, rest), F32)], -1)
    sin_h = jnp.concatenate([-sin, sin, jnp.zeros((seq, rest), F32)], -1)
    reps = LANES // MOBA_DH
    return jnp.tile(cos_h, (1, reps)), jnp.tile(sin_h, (1, reps))


def _head_rows(v):
    col = jnp.pad(v.astype(F32), (0, SUBLANES - v.shape[0]))
    return jnp.broadcast_to(col[:, None], (SUBLANES, LANES))


def kernel(x, w_in, gdn_conv_w, gdn_a_log, gdn_dt_bias, gdn_norm_g, w_out, ln1_g, ln1_b,
           w_up, ffn_conv_w, ffn_conv_b, w_down, ln2_g, ln2_b):
    batch, seq, d = x.shape
    t = batch * seq
    tm = min(512, seq)
    tl = min(1024, seq)
    tf = 256
    cos_t, sin_t = _rope_tables(seq)
    g_off = 3 * GDN_W
    z_off = g_off + 2 * GDN_HEADS
    x2 = x.reshape(t, d)
    for l in range(DEPTH):
        w = w_in[l]
        w_main = jnp.concatenate([w[:, :g_off], w[:, z_off:]], axis=1).astype(BF16)
        w_gate = jnp.pad(w[:, g_off:z_off], ((0, 0), (0, LANES - 2 * GDN_HEADS))).astype(BF16)
        qkva, z, gates, qkvb = _inproj(x2, w_main, w_gate, tm)
        o_a = _gdn(qkva, z, gates, gdn_conv_w[l], _head_rows(gdn_a_log[l]), _head_rows(gdn_dt_bias[l]),
                   gdn_norm_g[l][None, :], batch, seq, tl)
        o_b = _moba(qkvb, cos_t, sin_t, batch, seq)
        x2 = _mixffn(o_a, o_b, x2, w_out[l].astype(BF16), ln1_g[l][None, :], ln1_b[l][None, :],
                     w_up[l].astype(BF16), ffn_conv_w[l], ffn_conv_b[l][None, :], w_down[l].astype(BF16),
                     ln2_g[l][None, :], ln2_b[l][None, :], seq, tm, tf)
    return x2.reshape(batch, seq, d)
```

```python
import functools

import jax
import jax.numpy as jnp
import numpy as np
from jax import lax
from jax.experimental import pallas as pl
from jax.experimental.pallas import tpu as pltpu

F32 = jnp.float32
BF16 = jnp.bfloat16

D_MODEL = 1024
DEPTH = 2
GDN_HEADS = 4
GDN_DK = 128
GDN_DV = 128
GDN_CONV = 4
GDN_CHUNK = 64
MOBA_HEADS = 8
MOBA_DH = 64
MOBA_BLOCK = 256
MOBA_TOPK = 3
ROPE_DIMS = MOBA_DH // 4
ROPE_THETA = 500000.0
GDN_W = GDN_HEADS * GDN_DK
MOBA_W = MOBA_HEADS * MOBA_DH
D_FF = 2816
FFN_CONV = 3
DEEPNORM_ALPHA = (2 * DEPTH) ** 0.25
LN_EPS = 1e-5
NORM_EPS = 1e-6

LANES = 128
SUBLANES = 8
NEG_BIG = -1e30
LOG2E = 1.4426950408889634
VMEM_LIMIT = 56 * 1024 * 1024


def _dot(a, b, precision=None):
    return jnp.dot(a, b, preferred_element_type=F32, precision=precision)


def _dot_nt(a, b, precision=None):
    return lax.dot_general(a, b, (((1,), (1,)), ((), ())), preferred_element_type=F32,
                           precision=precision)


def _dot_tn(a, b, precision=None):
    return lax.dot_general(a, b, (((0,), (0,)), ((), ())), preferred_element_type=F32,
                           precision=precision)


def _split2(x):
    hi = x.astype(BF16)
    return hi, (x - hi.astype(F32)).astype(BF16)


def _split3(x):
    hi = x.astype(BF16)
    r1 = x - hi.astype(F32)
    mid = r1.astype(BF16)
    return hi, mid, (r1 - mid.astype(F32)).astype(BF16)


def _dot3(a, b):
    return _dot(a[0], b[0]) + (_dot(a[0], b[1]) + _dot(a[1], b[0]))


def _sigmoid(x):
    return 1.0 / (1.0 + jnp.exp(-x))


def _silu(x):
    return x * _sigmoid(x)


def _layer_norm(y, g, b):
    mu = jnp.mean(y, axis=-1, keepdims=True)
    d = y - mu
    var = jnp.mean(d * d, axis=-1, keepdims=True)
    return d * lax.rsqrt(var + LN_EPS) * g + b


def _const_spec(shape):
    nd = len(shape)
    return pl.BlockSpec(shape, lambda *_: (0,) * nd, pipeline_mode=pl.Buffered(1))


def _inproj_kernel(x_ref, wm_ref, wg_ref, qkva_ref, z_ref, gates_ref, qkvb_ref):
    xb = x_ref[...].astype(BF16)
    a_w = 3 * GDN_W
    qkva_ref[...] = _dot(xb, wm_ref[:, 0:a_w])
    z_ref[...] = _dot(xb, wm_ref[:, a_w:a_w + GDN_W])
    qkvb_ref[...] = _dot(xb, wm_ref[:, a_w + GDN_W:])
    gates_ref[...] = _dot(xb, wg_ref[...])


def _inproj(x2, w_main, w_gate, tm):
    t = x2.shape[0]
    a_w = 3 * GDN_W
    b_w = 3 * MOBA_W
    row = lambda w: pl.BlockSpec((tm, w), lambda i: (i, 0))
    return pl.pallas_call(
        _inproj_kernel,
        grid=(t // tm,),
        in_specs=[row(D_MODEL), _const_spec(w_main.shape), _const_spec(w_gate.shape)],
        out_specs=[row(a_w), row(GDN_W), row(LANES), row(b_w)],
        out_shape=[jax.ShapeDtypeStruct((t, a_w), F32), jax.ShapeDtypeStruct((t, GDN_W), F32),
                   jax.ShapeDtypeStruct((t, LANES), F32), jax.ShapeDtypeStruct((t, b_w), F32)],
        compiler_params=pltpu.CompilerParams(dimension_semantics=("arbitrary",),
                                             vmem_limit_bytes=VMEM_LIMIT),
        name="inproj",
    )(x2, w_main, w_gate)


def _gdn_kernel(qkv_ref, z_ref, gates_ref, cw_ref, alog_ref, dtb_ref, ng_ref, o_ref,
                xbuf, gact, qs, ks, vs, u_s, w_s, qd_s, kd_s, at_s, egl_s, st_ref, o_s, *, tl, cpi):
    c = GDN_CHUNK
    nc = tl // c
    hist = SUBLANES

    @pl.when(pl.program_id(1) == 0)
    def _():
        xbuf[0:hist, :] = jnp.zeros((hist, 3 * GDN_W), F32)
        st_ref[...] = jnp.zeros_like(st_ref)

    xbuf[hist:hist + tl, :] = qkv_ref[...]

    g8 = gates_ref[...].T[0:2 * GDN_HEADS, :]
    hrow = lax.broadcasted_iota(jnp.int32, g8.shape, 0)
    sp_in = g8 + dtb_ref[:, 0:1]
    softplus = jnp.maximum(sp_in, 0.0) + jnp.log(1.0 + jnp.exp(-jnp.abs(sp_in)))
    act = jnp.where(hrow < GDN_HEADS, -jnp.exp(alog_ref[:, 0:1]) * softplus, _sigmoid(g8))
    gact[...] = jnp.concatenate([act, jnp.zeros((LANES - 2 * GDN_HEADS, tl), F32)], axis=0).T

    for part, dst in enumerate((qs, ks, vs)):
        for h in range(GDN_HEADS):
            c0 = part * GDN_W + h * GDN_DK
            acc = None
            for j in range(GDN_CONV):
                r0 = hist - (GDN_CONV - 1) + j
                term = xbuf[r0:r0 + tl, c0:c0 + GDN_DK] * cw_ref[j:j + 1, c0:c0 + GDN_DK]
                acc = term if acc is None else acc + term
            y = _silu(acc)
            if part < 2:
                y = y * lax.rsqrt(jnp.sum(y * y, axis=-1, keepdims=True) + NORM_EPS)
            if part == 0:
                y = y * (GDN_DK ** -0.5)
            dst[:, h * GDN_DK:(h + 1) * GDN_DK] = y
    xbuf[0:hist, :] = xbuf[tl:tl + hist, :]

    ii = lax.broadcasted_iota(jnp.int32, (c, LANES), 0)
    lane = lax.broadcasted_iota(jnp.int32, (c, LANES), 1)
    jj = lane % c
    left = lane < c
    mask_l = jnp.where(left, 1.0, 0.0).astype(BF16)
    mask_r = jnp.where(left, 0.0, 1.0).astype(BF16)
    eye_p = jnp.where(ii == jj, 1.0, 0.0)
    i3 = lax.broadcasted_iota(jnp.int32, (c, 3 * c), 0)
    t3 = lax.broadcasted_iota(jnp.int32, (c, 3 * c), 1) % c
    ltri3 = jnp.where(i3 >= t3, 1.0, 0.0).astype(BF16)

    def blockdiag(p):
        return jnp.concatenate([p * mask_l, p * mask_r], axis=0)

    def blockdiag_wide(x):
        n = x.shape[1] // 2
        z = jnp.zeros((c, n), BF16)
        return jnp.concatenate([jnp.concatenate([x[:, :n], z], axis=1),
                                jnp.concatenate([z, x[:, n:]], axis=1)], axis=0)

    def dot3(a, b_hi, b_lo):
        lhs = jnp.concatenate([a[0], a[1], a[0]], axis=1)
        return _dot(lhs, jnp.concatenate([b_hi, b_hi, b_lo], axis=0))

    def pair_gates(rows, p):
        gblk = gact[rows, :]
        gc_all = _dot(ltri3, jnp.concatenate(_split3(gblk), axis=0))
        bc = lambda a, l: jnp.broadcast_to(a[:, l:l + 1], (c, LANES))
        gc_cat = jnp.concatenate([bc(gc_all, 2 * p), bc(gc_all, 2 * p + 1)], axis=1)
        b_cat = jnp.concatenate([bc(gblk, GDN_HEADS + 2 * p), bc(gblk, GDN_HEADS + 2 * p + 1)], axis=1)
        return gc_cat, b_cat

    def rows_at(ci):
        return pl.ds(ci * c, c)

    def phase1(it):
        chains = [(it * cpi + cc, p) for cc in range(cpi) for p in range(GDN_HEADS // 2)]
        t_mats, pws, rhs = [], [], []
        for ci, p in chains:
            rows = rows_at(ci)
            cols = slice(p * 2 * GDN_DK, (p + 1) * 2 * GDN_DK)
            gc_cat, b_cat = pair_gates(rows, p)
            gc_p = jnp.where(left, gc_cat[:, :LANES], gc_cat[:, LANES:])
            gcr_p = jnp.sum(jnp.where(ii == jj, gc_p, 0.0), axis=0, keepdims=True)
            decay = jnp.exp(jnp.minimum(gc_p - gcr_p, 0.0))
            q = qs[rows, cols]
            k = ks[rows, cols]
            kb = k * b_cat
            kq = _dot_nt(jnp.concatenate([kb.astype(BF16), q.astype(BF16)], axis=0),
                         blockdiag_wide(k.astype(BF16)))
            a_mat = jnp.where(ii > jj, kq[:c] * decay, 0.0)
            at_s[p, rows, :] = jnp.where(ii >= jj, kq[c:] * decay, 0.0).astype(BF16)
            gl = gc_cat[c - 1:c, :]
            egc = jnp.exp(gc_cat)
            qd_s[p, rows, :] = (q * egc).astype(BF16)
            kd_s[p * nc + ci] = (k * jnp.exp(gl - gc_cat)).T.astype(BF16)
            egl_s[p * nc + ci] = jnp.broadcast_to(jnp.exp(gl), (SUBLANES, 2 * LANES))
            vb = vs[rows, cols] * b_cat
            ke = kb * egc
            x = _split2(jnp.concatenate([vb[:, :GDN_DV], ke[:, :GDN_DK], vb[:, GDN_DV:], ke[:, GDN_DK:]],
                                        axis=1))
            rhs.append((blockdiag_wide(x[0]), blockdiag_wide(x[1])))
            t_mats.append(eye_p - a_mat)
            pws.append(_split2(a_mat))
        yield
        for _ in range(5):
            pws = [_split2(dot3(pw, blockdiag(pw[0]), blockdiag(pw[1]))) for pw in pws]
            t_mats = [t + dot3(_split2(t), blockdiag(pw[0]), blockdiag(pw[1]))
                      for t, pw in zip(t_mats, pws)]
            yield
        for (ci, p), t_mat, x in zip(chains, t_mats, rhs):
            rows = rows_at(ci)
            sol = dot3(_split2(t_mat), x[0], x[1])
            u_s[p, rows, :] = jnp.concatenate([sol[:, 0:128], sol[:, 256:384]], axis=1)
            w_s[p, rows, :] = jnp.concatenate([sol[:, 128:256], sol[:, 384:512]], axis=1).astype(BF16)

    zs = jnp.zeros((GDN_DK, GDN_DV), BF16)

    def phase2(ci):
        rows = rows_at(ci)
        pairs = range(GDN_HEADS // 2)
        sts = [(st_ref[2 * p], st_ref[2 * p + 1]) for p in pairs]
        rs = []
        for p in pairs:
            bds = jnp.concatenate([jnp.concatenate([sts[p][0].astype(BF16), zs], axis=1),
                                   jnp.concatenate([zs, sts[p][1].astype(BF16)], axis=1)], axis=0)
            rs.append(_dot(jnp.concatenate([w_s[p, rows, :], qd_s[p, rows, :]], axis=0), bds))
        vnbs = [(u_s[p, rows, :] - rs[p][:c]).astype(BF16) for p in pairs]
        upds = [_dot(kd_s[p * nc + ci], vnbs[p]) for p in pairs]
        for p in pairs:
            o_s[rows, p * 2 * GDN_DV:(p + 1) * 2 * GDN_DV] = (
                rs[p][c:] + _dot(at_s[p, rows, :], blockdiag_wide(vnbs[p])))
        for p in pairs:
            egl = egl_s[p * nc + ci][0:1, :]
            st_ref[2 * p] = sts[p][0] * egl[:, :GDN_DV] + upds[p][:GDN_DK, :GDN_DV]
            st_ref[2 * p + 1] = sts[p][1] * egl[:, GDN_DV:] + upds[p][GDN_DK:, GDN_DV:]

    ready = []
    for it in range(nc // cpi):
        for _ in phase1(it):
            if ready:
                phase2(ready.pop(0))
        ready.extend(range(it * cpi, (it + 1) * cpi))
    for ci in ready:
        phase2(ci)

    for h in range(GDN_HEADS):
        cols = slice(h * GDN_DV, (h + 1) * GDN_DV)
        o = o_s[:, cols]
        o = o * lax.rsqrt(jnp.mean(o * o, axis=-1, keepdims=True) + NORM_EPS) * ng_ref[...]
        o_ref[:, cols] = (o * _silu(z_ref[:, cols])).astype(o_ref.dtype)


def _gdn(qkva, z, gates, conv_w, alog_row, dtb_row, ng_row, batch, seq, tl):
    t = batch * seq
    nt = seq // tl
    nc = tl // GDN_CHUNK
    a_w = 3 * GDN_W
    row = lambda w: pl.BlockSpec((tl, w), lambda b, i: (b * nt + i, 0))
    npair = GDN_HEADS // 2
    pshape = (npair, tl, 2 * GDN_DK)
    return pl.pallas_call(
        functools.partial(_gdn_kernel, tl=tl, cpi=min(4, nc)),
        grid=(batch, nt),
        in_specs=[row(a_w), row(GDN_W), row(LANES), _const_spec(conv_w.shape),
                  _const_spec(alog_row.shape), _const_spec(dtb_row.shape), _const_spec(ng_row.shape)],
        out_specs=row(GDN_W),
        out_shape=jax.ShapeDtypeStruct((t, GDN_W), BF16),
        scratch_shapes=[
            pltpu.VMEM((tl + SUBLANES, a_w), F32),
            pltpu.VMEM((tl, LANES), F32),
            pltpu.VMEM((tl, GDN_W), F32),
            pltpu.VMEM((tl, GDN_W), F32),
            pltpu.VMEM((tl, GDN_W), F32),
            pltpu.VMEM(pshape, F32),
            pltpu.VMEM(pshape, BF16),
            pltpu.VMEM(pshape, BF16),
            pltpu.VMEM((npair * nc, 2 * GDN_DK, GDN_CHUNK), BF16),
            pltpu.VMEM((npair, tl, 2 * GDN_CHUNK), BF16),
            pltpu.VMEM((npair * nc, SUBLANES, 2 * LANES), F32),
            pltpu.VMEM((GDN_HEADS, GDN_DK, GDN_DV), F32),
            pltpu.VMEM((tl, GDN_W), F32),
        ],
        compiler_params=pltpu.CompilerParams(dimension_semantics=("arbitrary", "arbitrary"),
                                             vmem_limit_bytes=VMEM_LIMIT),
        name="gdn",
    )(qkva, z, gates, conv_w, alog_row, dtb_row, ng_row)


def _rope(x, cos_t, sin_t):
    half = ROPE_DIMS // 2
    lane = lax.broadcasted_iota(jnp.int32, x.shape, 1)
    up = pltpu.roll(x, LANES - half, axis=1)
    dn = pltpu.roll(x, half, axis=1)
    partner = jnp.where((lane % MOBA_DH) < half, up, dn)
    return x * cos_t + partner * sin_t


def _moba_kernel(q_ref, k_ref, v_ref, cos_ref, sin_ref, o_ref, ka_s, vb_s, kmean_s,
                 qa_s, s_all, m_run, acc_s, *, nb):
    blk = MOBA_BLOCK
    nbp = -(-nb // SUBLANES) * SUBLANES
    assert nbp <= MOBA_DH, "block-selection columns must fit beside one head's features"
    lane = lax.broadcasted_iota(jnp.int32, (blk, LANES), 1)
    head_a = lane < MOBA_DH
    sel_off = (MOBA_DH, 0)
    scale = MOBA_DH ** -0.5

    kmean_s[...] = jnp.zeros_like(kmean_s)

    def prep(it, carry):
        ns = [2 * it, 2 * it + 1]
        rows = [pl.ds(pl.multiple_of(n * blk, blk), blk) for n in ns]
        krs = [_rope(k_ref[r, :], cos_ref[r, :], sin_ref[r, :]) for r in rows]
        for n, r, kr in zip(ns, rows, krs):
            ka_s[0, r, :] = jnp.where(head_a, kr, jnp.where(lane == sel_off[0] + n, 1.0, 0.0)).astype(BF16)
            ka_s[1, r, :] = jnp.where(head_a, jnp.where(lane == sel_off[1] + n, 1.0, 0.0), kr).astype(BF16)
            v = v_ref[r, :]
            vb_s[0, r, :] = jnp.where(head_a, v, 1.0).astype(BF16)
            vb_s[1, r, :] = jnp.where(head_a, 1.0, v).astype(BF16)
            kmean_s[pl.ds(n, 1), :] = jnp.mean(kr, axis=0, keepdims=True)
        return carry

    lax.fori_loop(0, nb // 2, prep, 0)

    km = kmean_s[0:nbp, :]
    km_a = jnp.where(lax.broadcasted_iota(jnp.int32, (nbp, LANES), 1) < MOBA_DH, km, 0.0)

    def stack3(m):
        hi, lo = _split2(m)
        return jnp.concatenate([hi, lo, hi], axis=1)

    kmean3 = (stack3(km_a), stack3(km - km_a))
    brow = lax.broadcasted_iota(jnp.int32, (nbp, blk), 0)
    gq = 4 if nb % 4 == 0 else 2
    gap = jnp.zeros((MOBA_DH - nbp, blk), F32)

    def gating(it, carry):
        items, gs, sels = [], [], []
        for d in range(gq):
            i = gq * it + d
            qrows = pl.ds(pl.multiple_of(i * blk, blk), blk)
            q = _rope(q_ref[qrows, :], cos_ref[qrows, :], sin_ref[qrows, :])
            q_hi, q_lo = _split2(q)
            q3 = jnp.concatenate([q_hi, q_hi, q_lo], axis=1)
            items.append((i, qrows, q))
            for h in range(2):
                gs.append(jnp.where(brow < i, _dot_nt(kmean3[h], q3), -jnp.inf))
                sels.append(brow == i)
        for r in range(MOBA_TOPK):
            for n in range(len(gs)):
                i = items[n // 2][0]
                g = gs[n]
                m = jnp.max(g, axis=0, keepdims=True)
                first = jnp.min(jnp.where(g == m, brow, nbp), axis=0, keepdims=True)
                pick = brow == first
                sels[n] = jnp.logical_or(sels[n], jnp.logical_and(pick, r < i))
                gs[n] = jnp.where(pick, -jnp.inf, g)
        for n, (i, qrows, q) in enumerate(items):
            bias_a = jnp.where(sels[2 * n], 0.0, NEG_BIG)
            bias_b = jnp.where(sels[2 * n + 1], 0.0, NEG_BIG)
            bias = jnp.concatenate([bias_b, gap, bias_a, gap], axis=0).T
            qs_ = q * scale
            qa_s[0, qrows, :] = jnp.where(head_a, qs_, bias).astype(BF16)
            qa_s[1, qrows, :] = jnp.where(head_a, bias, qs_).astype(BF16)
        return carry

    lax.fori_loop(0, nb // gq, gating, 0)

    sb = 2 * blk
    causal = (lax.broadcasted_iota(jnp.int32, (sb, sb), 1)
              <= lax.broadcasted_iota(jnp.int32, (sb, sb), 0))
    head_a2 = lax.broadcasted_iota(jnp.int32, (sb, LANES), 1) < MOBA_DH
    ntile = sb // LANES

    def rows_of(idx):
        return pl.ds(pl.multiple_of(idx * sb, sb), sb)

    def row_max(s):
        parts = [s[:, t * LANES:(t + 1) * LANES] for t in range(ntile)]
        return jnp.maximum(jnp.maximum(parts[0], parts[1]), jnp.maximum(parts[2], parts[3]))

    def loop_by_two(n, steps):
        def body(t, c):
            steps(2 * t, 2)
            return c
        lax.fori_loop(0, n // 2, body, 0)

        @pl.when(n % 2 == 1)
        def _():
            steps(n - 1, 1)

    def qpair(qi, carry):
        qrows = rows_of(qi)

        for h in range(2):
            m_run[h] = jnp.full((sb, LANES), NEG_BIG, F32)

        def pass1(j0, count):
            dots = [(h, j0 + d, _dot_nt(qa_s[h, qrows, :], ka_s[h, rows_of(j0 + d), :]))
                    for d in range(count) for h in range(2)]
            for h, j, s in dots:
                s = s * LOG2E
                s_all[h, j] = s
                m_run[h] = jnp.maximum(m_run[h], row_max(s))

        loop_by_two(qi, pass1)
        for h in range(2):
            s = jnp.where(causal, _dot_nt(qa_s[h, qrows, :], ka_s[h, qrows, :]) * LOG2E, NEG_BIG)
            s_all[h, qi] = s
            m_run[h] = jnp.maximum(m_run[h], row_max(s))

        for h in range(2):
            m_run[h] = jnp.broadcast_to(jnp.max(m_run[h], axis=-1, keepdims=True), (sb, LANES))
            acc_s[h] = jnp.zeros((sb, LANES), F32)

        def pass2(j0, count):
            krows = pl.ds(pl.multiple_of(j0 * sb, sb), count * sb)
            for h in range(2):
                mb = m_run[h]
                ps = [jnp.exp2(s_all[h, j0 + d, :, t * LANES:(t + 1) * LANES] - mb)
                      for d in range(count) for t in range(ntile)]
                acc_s[h] += _dot(jnp.concatenate(ps, axis=1).astype(BF16), vb_s[h, krows, :])

        loop_by_two(qi + 1, pass2)
        acc_a, acc_b = acc_s[0], acc_s[1]
        o_a = acc_a / jnp.max(jnp.where(head_a2, -jnp.inf, acc_a), axis=-1, keepdims=True)
        o_b = acc_b / jnp.max(jnp.where(head_a2, acc_b, -jnp.inf), axis=-1, keepdims=True)
        o_ref[qrows, :] = jnp.where(head_a2, o_a, o_b).astype(o_ref.dtype)
        return carry

    lax.fori_loop(0, nb // 2, qpair, 0)


def _moba(qkvb, cos_t, sin_t, batch, seq):
    t = batch * seq
    nb = seq // MOBA_BLOCK
    assert nb % 2 == 0, "query and key blocks are processed in pairs"
    sb = 2 * MOBA_BLOCK
    npair = MOBA_W // LANES
    col = lambda off: pl.BlockSpec((seq, LANES), lambda b, p: (b, off + p))
    return pl.pallas_call(
        functools.partial(_moba_kernel, nb=nb),
        grid=(batch, npair),
        in_specs=[col(0), col(npair), col(2 * npair), _const_spec(cos_t.shape), _const_spec(sin_t.shape)],
        out_specs=col(0),
        out_shape=jax.ShapeDtypeStruct((t, MOBA_W), BF16),
        scratch_shapes=[
            pltpu.VMEM((2, seq, LANES), BF16),
            pltpu.VMEM((2, seq, LANES), BF16),
            pltpu.VMEM((LANES, LANES), F32),
            pltpu.VMEM((2, seq, LANES), BF16),
            pltpu.VMEM((2, nb // 2, sb, sb), F32),
            pltpu.VMEM((2, sb, LANES), F32),
            pltpu.VMEM((2, sb, LANES), F32),
        ],
        compiler_params=pltpu.CompilerParams(dimension_semantics=("arbitrary", "arbitrary"),
                                             vmem_limit_bytes=VMEM_LIMIT),
        name="moba",
    )(qkvb, qkvb, qkvb, cos_t, sin_t)


def _mixffn_kernel(oa_ref, ob_ref, x_ref, wo_ref, g1_ref, b1_ref, wup_ref, cw_ref, cb_ref, wdn_ref,
                   g2_ref, b2_ref, o_ref, buf_g, buf_v, carry_g, carry_v, hid_s, *, tm, seq, tf):
    nf = D_FF // tf
    hist = SUBLANES
    seq_start = (pl.program_id(0) * tm) % seq == 0
    mix = _dot(oa_ref[...], wo_ref[0:GDN_W, :]) + _dot(ob_ref[...], wo_ref[GDN_W:, :])
    x1 = _layer_norm(DEEPNORM_ALPHA * x_ref[...] + mix, g1_ref[...], b1_ref[...])
    xb = x1.astype(BF16)
    for f in range(nf):
        halves = []
        for buf, carry, off in ((buf_g, carry_g, 0), (buf_v, carry_v, D_FF)):
            cols = slice(off + f * tf, off + (f + 1) * tf)
            up = _dot(xb, wup_ref[:, cols])
            buf[0:hist, :] = jnp.where(seq_start, 0.0, carry[f])
            buf[hist:hist + tm, :] = up
            carry[f] = up[tm - hist:tm, :]
            y = up * cw_ref[FFN_CONV - 1:FFN_CONV, cols]
            for j in range(FFN_CONV - 1):
                r0 = hist - (FFN_CONV - 1) + j
                y = y + buf[r0:r0 + tm, :] * cw_ref[j:j + 1, cols]
            halves.append(y + cb_ref[:, cols])
        hid_s[:, f * tf:(f + 1) * tf] = (_silu(halves[0]) * halves[1]).astype(BF16)
    y = DEEPNORM_ALPHA * x1 + _dot(hid_s[...], wdn_ref[...])
    o_ref[...] = _layer_norm(y, g2_ref[...], b2_ref[...])


def _mixffn(o_a, o_b, x2, w_out, g1, b1, w_up, conv_w, conv_b, w_down, g2, b2, seq, tm, tf):
    t = x2.shape[0]
    nf = D_FF // tf
    row = lambda w: pl.BlockSpec((tm, w), lambda i: (i, 0))
    consts = (w_out, g1, b1, w_up, conv_w, conv_b, w_down, g2, b2)
    return pl.pallas_call(
        functools.partial(_mixffn_kernel, tm=tm, seq=seq, tf=tf),
        grid=(t // tm,),
        in_specs=[row(GDN_W), row(MOBA_W), row(D_MODEL)] + [_const_spec(c.shape) for c in consts],
        out_specs=row(D_MODEL),
        out_shape=jax.ShapeDtypeStruct((t, D_MODEL), F32),
        scratch_shapes=[
            pltpu.VMEM((tm + SUBLANES, tf), F32), pltpu.VMEM((tm + SUBLANES, tf), F32),
            pltpu.VMEM((nf, SUBLANES, tf), F32), pltpu.VMEM((nf, SUBLANES, tf), F32),
            pltpu.VMEM((tm, D_FF), BF16),
        ],
        compiler_params=pltpu.CompilerParams(dimension_semantics=("arbitrary",),
                                             vmem_limit_bytes=VMEM_LIMIT),
        name="mixffn",
    )(o_a, o_b, x2, *consts)


def _rope_tables(seq):
    half = ROPE_DIMS // 2
    inv = ROPE_THETA ** (-jnp.arange(half, dtype=F32) / half)
    ang = jnp.arange(seq, dtype=jnp.int32).astype(F32)[:, None] * inv[None, :]
    cos, sin = jnp.cos(ang), jnp.sin(ang)
    rest = MOBA_DH - ROPE_DIMS
    cos_h = jnp.concatenate([cos, cos, jnp.ones((seq, rest), F32)], -1)
    sin_h = jnp.concatenate([-sin, sin, jnp.zeros((seq, rest), F32)], -1)
    reps = LANES // MOBA_DH
    return jnp.tile(cos_h, (1, reps)), jnp.tile(sin_h, (1, reps))


def _head_rows(v):
    col = jnp.pad(v.astype(F32), (0, SUBLANES - v.shape[0]))
    return jnp.broadcast_to(col[:, None], (SUBLANES, LANES))


def kernel(x, w_in, gdn_conv_w, gdn_a_log, gdn_dt_bias, gdn_norm_g, w_out, ln1_g, ln1_b,
           w_up, ffn_conv_w, ffn_conv_b, w_down, ln2_g, ln2_b):
    batch, seq, d = x.shape
    t = batch * seq
    tm = min(512, seq)
    tl = min(1024, seq)
    tf = 256
    cos_t, sin_t = _rope_tables(seq)
    g_off = 3 * GDN_W
    z_off = g_off + 2 * GDN_HEADS
    x2 = x.reshape(t, d)
    for l in range(DEPTH):
        w = w_in[l]
        w_main = jnp.concatenate([w[:, :g_off], w[:, z_off:]], axis=1).astype(BF16)
        w_gate = jnp.pad(w[:, g_off:z_off], ((0, 0), (0, LANES - 2 * GDN_HEADS))).astype(BF16)
        qkva, z, gates, qkvb = _inproj(x2, w_main, w_gate, tm)
        o_a = _gdn(qkva, z, gates, gdn_conv_w[l], _head_rows(gdn_a_log[l]), _head_rows(gdn_dt_bias[l]),
                   gdn_norm_g[l][None, :], batch, seq, tl)
        o_b = _moba(qkvb, cos_t, sin_t, batch, seq)
        x2 = _mixffn(o_a, o_b, x2, w_out[l].astype(BF16), ln1_g[l][None, :], ln1_b[l][None, :],
                     w_up[l].astype(BF16), ffn_conv_w[l], ffn_conv_b[l][None, :], w_down[l].astype(BF16),
                     ln2_g[l][None, :], ln2_b[l][None, :], seq, tm, tf)
    return x2.reshape(batch, seq, d)
```

```python
import functools

import jax
import jax.numpy as jnp
from jax import lax
from jax.experimental import pallas as pl
from jax.experimental.pallas import tpu as pltpu

F32 = jnp.float32
BF16 = jnp.bfloat16

D_MODEL = 1024
DEPTH = 2
GDN_HEADS = 4
GDN_DK = 128
GDN_DV = 128
GDN_CONV = 4
GDN_CHUNK = 64
GDN_SOLVE_GROUP = 4
MOBA_HEADS = 8
MOBA_DH = 64
MOBA_BLOCK = 256
MOBA_TOPK = 3
ROPE_DIMS = MOBA_DH // 4
ROPE_THETA = 500000.0
GDN_W = GDN_HEADS * GDN_DK
MOBA_W = MOBA_HEADS * MOBA_DH
D_FF = 2816
FFN_CONV = 3
DEEPNORM_ALPHA = (2 * DEPTH) ** 0.25
LN_EPS = 1e-5
NORM_EPS = 1e-6

LANES = 128
SUBLANES = 8
NEG_BIG = -1e30
LOG2E = 1.4426950408889634
VMEM_LIMIT = 56 * 1024 * 1024


def _dot(a, b):
    return jnp.dot(a, b, preferred_element_type=F32)


def _dot_nt(a, b):
    return lax.dot_general(a, b, (((1,), (1,)), ((), ())), preferred_element_type=F32)


def _split2(x):
    hi = x.astype(BF16)
    return hi, (x - hi.astype(F32)).astype(BF16)


def _split3(x):
    hi = x.astype(BF16)
    r1 = x - hi.astype(F32)
    mid = r1.astype(BF16)
    return hi, mid, (r1 - mid.astype(F32)).astype(BF16)


def _sigmoid(x):
    return 1.0 / (1.0 + jnp.exp(-x))


def _silu(x):
    return x * _sigmoid(x)


def _layer_norm(y, g, b):
    mu = jnp.mean(y, axis=-1, keepdims=True)
    d = y - mu
    var = jnp.mean(d * d, axis=-1, keepdims=True)
    return d * lax.rsqrt(var + LN_EPS) * g + b


def _const_spec(shape):
    nd = len(shape)
    return pl.BlockSpec(shape, lambda *_: (0,) * nd, pipeline_mode=pl.Buffered(1))


def _inproj_kernel(x_ref, wm_ref, wg_ref, qkva_ref, z_ref, gates_ref, qkvb_ref):
    xb = x_ref[...].astype(BF16)
    a_w = 3 * GDN_W
    qkva_ref[...] = _dot(xb, wm_ref[:, 0:a_w])
    z_ref[...] = _dot(xb, wm_ref[:, a_w:a_w + GDN_W])
    qkvb_ref[...] = _dot(xb, wm_ref[:, a_w + GDN_W:])
    gates_ref[...] = _dot(xb, wg_ref[...])


def _inproj(x2, w_main, w_gate, tm):
    t = x2.shape[0]
    a_w = 3 * GDN_W
    b_w = 3 * MOBA_W
    row = lambda w: pl.BlockSpec((tm, w), lambda i: (i, 0))
    return pl.pallas_call(
        _inproj_kernel,
        grid=(t // tm,),
        in_specs=[row(D_MODEL), _const_spec(w_main.shape), _const_spec(w_gate.shape)],
        out_specs=[row(a_w), row(GDN_W), row(LANES), row(b_w)],
        out_shape=[jax.ShapeDtypeStruct((t, a_w), F32), jax.ShapeDtypeStruct((t, GDN_W), F32),
                   jax.ShapeDtypeStruct((t, LANES), F32), jax.ShapeDtypeStruct((t, b_w), F32)],
        compiler_params=pltpu.CompilerParams(dimension_semantics=("arbitrary",),
                                             vmem_limit_bytes=VMEM_LIMIT),
        name="inproj",
    )(x2, w_main, w_gate)


def _gdn_kernel(qkv_ref, z_ref, gates_ref, cw_ref, alog_ref, dtb_ref, ng_ref, o_ref,
                xbuf, gact, qs, ks, vs, u_s, w_s, qd_s, kd_s, at_s, egl_s, st_ref, o_s, *, tl, cpi):
    c = GDN_CHUNK
    nc = tl // c
    hist = SUBLANES

    @pl.when(pl.program_id(1) == 0)
    def _():
        xbuf[0:hist, :] = jnp.zeros((hist, 3 * GDN_W), F32)
        st_ref[...] = jnp.zeros_like(st_ref)

    xbuf[hist:hist + tl, :] = qkv_ref[...]

    g8 = gates_ref[...].T[0:2 * GDN_HEADS, :]
    hrow = lax.broadcasted_iota(jnp.int32, g8.shape, 0)
    sp_in = g8 + dtb_ref[:, 0:1]
    softplus = jnp.maximum(sp_in, 0.0) + jnp.log(1.0 + jnp.exp(-jnp.abs(sp_in)))
    act = jnp.where(hrow < GDN_HEADS, -jnp.exp(alog_ref[:, 0:1]) * softplus, _sigmoid(g8))
    gact[...] = jnp.concatenate([act, jnp.zeros((LANES - 2 * GDN_HEADS, tl), F32)], axis=0).T

    for part, dst in enumerate((qs, ks, vs)):
        for h in range(GDN_HEADS):
            c0 = part * GDN_W + h * GDN_DK
            acc = None
            for j in range(GDN_CONV):
                r0 = hist - (GDN_CONV - 1) + j
                term = xbuf[r0:r0 + tl, c0:c0 + GDN_DK] * cw_ref[j:j + 1, c0:c0 + GDN_DK]
                acc = term if acc is None else acc + term
            y = _silu(acc)
            if part < 2:
                y = y * lax.rsqrt(jnp.sum(y * y, axis=-1, keepdims=True) + NORM_EPS)
            if part == 0:
                y = y * (GDN_DK ** -0.5)
            dst[:, h * GDN_DK:(h + 1) * GDN_DK] = y
    xbuf[0:hist, :] = xbuf[tl:tl + hist, :]

    ii = lax.broadcasted_iota(jnp.int32, (c, LANES), 0)
    lane = lax.broadcasted_iota(jnp.int32, (c, LANES), 1)
    jj = lane % c
    left = lane < c
    mask_l = jnp.where(left, 1.0, 0.0).astype(BF16)
    mask_r = jnp.where(left, 0.0, 1.0).astype(BF16)
    eye_p = jnp.where(ii == jj, 1.0, 0.0)
    i3 = lax.broadcasted_iota(jnp.int32, (c, 3 * c), 0)
    t3 = lax.broadcasted_iota(jnp.int32, (c, 3 * c), 1) % c
    ltri3 = jnp.where(i3 >= t3, 1.0, 0.0).astype(BF16)

    def blockdiag(p):
        return jnp.concatenate([p * mask_l, p * mask_r], axis=0)

    def blockdiag_wide(x):
        n = x.shape[1] // 2
        z = jnp.zeros((c, n), BF16)
        return jnp.concatenate([jnp.concatenate([x[:, :n], z], axis=1),
                                jnp.concatenate([z, x[:, n:]], axis=1)], axis=0)

    def dot3(a, b_hi, b_lo):
        lhs = jnp.concatenate([a[0], a[1], a[0]], axis=1)
        return _dot(lhs, jnp.concatenate([b_hi, b_hi, b_lo], axis=0))

    def pair_gates(rows, p):
        gblk = gact[rows, :]
        gc_all = _dot(ltri3, jnp.concatenate(_split3(gblk), axis=0))
        bc = lambda a, l: jnp.broadcast_to(a[:, l:l + 1], (c, LANES))
        gc_cat = jnp.concatenate([bc(gc_all, 2 * p), bc(gc_all, 2 * p + 1)], axis=1)
        b_cat = jnp.concatenate([bc(gblk, GDN_HEADS + 2 * p), bc(gblk, GDN_HEADS + 2 * p + 1)], axis=1)
        return gc_cat, b_cat

    def rows_at(ci):
        return pl.ds(ci * c, c)

    def phase1(it):
        chains = [(it * cpi + cc, p) for cc in range(cpi) for p in range(GDN_HEADS // 2)]
        t_mats, pws, rhs = [], [], []
        for ci, p in chains:
            rows = rows_at(ci)
            cols = slice(p * 2 * GDN_DK, (p + 1) * 2 * GDN_DK)
            gc_cat, b_cat = pair_gates(rows, p)
            gc_p = jnp.where(left, gc_cat[:, :LANES], gc_cat[:, LANES:])
            gcr_p = jnp.sum(jnp.where(ii == jj, gc_p, 0.0), axis=0, keepdims=True)
            decay = jnp.exp(jnp.minimum(gc_p - gcr_p, 0.0))
            q = qs[rows, cols]
            k = ks[rows, cols]
            kb = k * b_cat
            kq = _dot_nt(jnp.concatenate([kb.astype(BF16), q.astype(BF16)], axis=0),
                         blockdiag_wide(k.astype(BF16)))
            a_mat = jnp.where(ii > jj, kq[:c] * decay, 0.0)
            at_s[p, rows, :] = jnp.where(ii >= jj, kq[c:] * decay, 0.0).astype(BF16)
            gl = gc_cat[c - 1:c, :]
            egc = jnp.exp(gc_cat)
            qd_s[p, rows, :] = (q * egc).astype(BF16)
            kd_s[p * nc + ci] = (k * jnp.exp(gl - gc_cat)).T.astype(BF16)
            egl_s[p * nc + ci] = jnp.broadcast_to(jnp.exp(gl), (SUBLANES, 2 * LANES))
            vb = vs[rows, cols] * b_cat
            ke = kb * egc
            x = _split2(jnp.concatenate([vb[:, :GDN_DV], ke[:, :GDN_DK], vb[:, GDN_DV:], ke[:, GDN_DK:]],
                                        axis=1))
            rhs.append((blockdiag_wide(x[0]), blockdiag_wide(x[1])))
            t_mats.append(eye_p - a_mat)
            pws.append(_split2(a_mat))
        yield
        for _ in range(5):
            pws = [_split2(dot3(pw, blockdiag(pw[0]), blockdiag(pw[1]))) for pw in pws]
            t_mats = [t + dot3(_split2(t), blockdiag(pw[0]), blockdiag(pw[1]))
                      for t, pw in zip(t_mats, pws)]
            yield
        for (ci, p), t_mat, x in zip(chains, t_mats, rhs):
            rows = rows_at(ci)
            sol = dot3(_split2(t_mat), x[0], x[1])
            hw = GDN_DV + GDN_DK
            u_s[p, rows, :] = jnp.concatenate([sol[:, :GDN_DV], sol[:, hw:hw + GDN_DV]], axis=1)
            w_s[p, rows, :] = jnp.concatenate([sol[:, GDN_DV:hw], sol[:, hw + GDN_DV:]], axis=1).astype(BF16)

    zs = jnp.zeros((GDN_DK, GDN_DV), BF16)

    def phase2(ci):
        rows = rows_at(ci)
        pairs = range(GDN_HEADS // 2)
        sts = [(st_ref[2 * p], st_ref[2 * p + 1]) for p in pairs]
        rs = []
        for p in pairs:
            bds = jnp.concatenate([jnp.concatenate([sts[p][0].astype(BF16), zs], axis=1),
                                   jnp.concatenate([zs, sts[p][1].astype(BF16)], axis=1)], axis=0)
            rs.append(_dot(jnp.concatenate([w_s[p, rows, :], qd_s[p, rows, :]], axis=0), bds))
        vnbs = [(u_s[p, rows, :] - rs[p][:c]).astype(BF16) for p in pairs]
        upds = [_dot(kd_s[p * nc + ci], vnbs[p]) for p in pairs]
        for p in pairs:
            o_s[rows, p * 2 * GDN_DV:(p + 1) * 2 * GDN_DV] = (
                rs[p][c:] + _dot(at_s[p, rows, :], blockdiag_wide(vnbs[p])))
        for p in pairs:
            egl = egl_s[p * nc + ci][0:1, :]
            st_ref[2 * p] = sts[p][0] * egl[:, :GDN_DV] + upds[p][:GDN_DK, :GDN_DV]
            st_ref[2 * p + 1] = sts[p][1] * egl[:, GDN_DV:] + upds[p][GDN_DK:, GDN_DV:]

    ready = []
    for it in range(nc // cpi):
        for _ in phase1(it):
            if ready:
                phase2(ready.pop(0))
        ready.extend(range(it * cpi, (it + 1) * cpi))
    for ci in ready:
        phase2(ci)

    for h in range(GDN_HEADS):
        cols = slice(h * GDN_DV, (h + 1) * GDN_DV)
        o = o_s[:, cols]
        o = o * lax.rsqrt(jnp.mean(o * o, axis=-1, keepdims=True) + NORM_EPS) * ng_ref[...]
        o_ref[:, cols] = (o * _silu(z_ref[:, cols])).astype(o_ref.dtype)


def _gdn(qkva, z, gates, conv_w, alog_row, dtb_row, ng_row, batch, seq, tl):
    t = batch * seq
    nt = seq // tl
    nc = tl // GDN_CHUNK
    a_w = 3 * GDN_W
    row = lambda w: pl.BlockSpec((tl, w), lambda b, i: (b * nt + i, 0))
    npair = GDN_HEADS // 2
    pshape = (npair, tl, 2 * GDN_DK)
    return pl.pallas_call(
        functools.partial(_gdn_kernel, tl=tl, cpi=min(GDN_SOLVE_GROUP, nc)),
        grid=(batch, nt),
        in_specs=[row(a_w), row(GDN_W), row(LANES), _const_spec(conv_w.shape),
                  _const_spec(alog_row.shape), _const_spec(dtb_row.shape), _const_spec(ng_row.shape)],
        out_specs=row(GDN_W),
        out_shape=jax.ShapeDtypeStruct((t, GDN_W), BF16),
        scratch_shapes=[
            pltpu.VMEM((tl + SUBLANES, a_w), F32),
            pltpu.VMEM((tl, LANES), F32),
            pltpu.VMEM((tl, GDN_W), F32),
            pltpu.VMEM((tl, GDN_W), F32),
            pltpu.VMEM((tl, GDN_W), F32),
            pltpu.VMEM(pshape, F32),
            pltpu.VMEM(pshape, BF16),
            pltpu.VMEM(pshape, BF16),
            pltpu.VMEM((npair * nc, 2 * GDN_DK, GDN_CHUNK), BF16),
            pltpu.VMEM((npair, tl, 2 * GDN_CHUNK), BF16),
            pltpu.VMEM((npair * nc, SUBLANES, 2 * LANES), F32),
            pltpu.VMEM((GDN_HEADS, GDN_DK, GDN_DV), F32),
            pltpu.VMEM((tl, GDN_W), F32),
        ],
        compiler_params=pltpu.CompilerParams(dimension_semantics=("arbitrary", "arbitrary"),
                                             vmem_limit_bytes=VMEM_LIMIT),
        name="gdn",
    )(qkva, z, gates, conv_w, alog_row, dtb_row, ng_row)


def _rope(x, cos_t, sin_t):
    half = ROPE_DIMS // 2
    lane = lax.broadcasted_iota(jnp.int32, x.shape, 1)
    up = pltpu.roll(x, LANES - half, axis=1)
    dn = pltpu.roll(x, half, axis=1)
    partner = jnp.where((lane % MOBA_DH) < half, up, dn)
    return x * cos_t + partner * sin_t


def _moba_kernel(q_ref, k_ref, v_ref, cos_ref, sin_ref, o_ref, ka_s, vb_s, kmean_s,
                 qa_s, s_all, m_run, acc_s, *, nb):
    blk = MOBA_BLOCK
    nbp = -(-nb // SUBLANES) * SUBLANES
    assert nbp <= MOBA_DH, "block-selection columns must fit beside one head's features"
    lane = lax.broadcasted_iota(jnp.int32, (blk, LANES), 1)
    head_a = lane < MOBA_DH
    sel_off = (MOBA_DH, 0)
    scale = MOBA_DH ** -0.5

    kmean_s[...] = jnp.zeros_like(kmean_s)

    kp = 4 if nb % 4 == 0 else 2

    def prep(it, carry):
        ns = [kp * it + d for d in range(kp)]
        rows = [pl.ds(pl.multiple_of(n * blk, blk), blk) for n in ns]
        krs = [_rope(k_ref[r, :], cos_ref[r, :], sin_ref[r, :]) for r in rows]
        for n, r, kr in zip(ns, rows, krs):
            ka_s[0, r, :] = jnp.where(head_a, kr, jnp.where(lane == sel_off[0] + n, 1.0, 0.0)).astype(BF16)
            ka_s[1, r, :] = jnp.where(head_a, jnp.where(lane == sel_off[1] + n, 1.0, 0.0), kr).astype(BF16)
            v = v_ref[r, :]
            vb_s[0, r, :] = jnp.where(head_a, v, 1.0).astype(BF16)
            vb_s[1, r, :] = jnp.where(head_a, 1.0, v).astype(BF16)
            kmean_s[pl.ds(n, 1), :] = jnp.mean(kr, axis=0, keepdims=True)
        return carry

    lax.fori_loop(0, nb // kp, prep, 0)

    km = kmean_s[0:nbp, :]
    km_a = jnp.where(lax.broadcasted_iota(jnp.int32, (nbp, LANES), 1) < MOBA_DH, km, 0.0)

    def stack3(m):
        hi, lo = _split2(m)
        return jnp.concatenate([hi, lo, hi], axis=1)

    kmean3 = (stack3(km_a), stack3(km - km_a))
    brow = lax.broadcasted_iota(jnp.int32, (nbp, blk), 0)
    gq = 8 if nb % 8 == 0 else 2
    gap = jnp.zeros((MOBA_DH - nbp, blk), F32)

    def gating(it, carry):
        items, gs, sels = [], [], []
        for d in range(gq):
            i = gq * it + d
            qrows = pl.ds(pl.multiple_of(i * blk, blk), blk)
            q = _rope(q_ref[qrows, :], cos_ref[qrows, :], sin_ref[qrows, :])
            q_hi, q_lo = _split2(q)
            q3 = jnp.concatenate([q_hi, q_hi, q_lo], axis=1)
            items.append((i, qrows, q))
            for h in range(2):
                gs.append(jnp.where(brow < i, _dot_nt(kmean3[h], q3), -jnp.inf))
                sels.append(brow == i)
        for r in range(MOBA_TOPK):
            for n in range(len(gs)):
                i = items[n // 2][0]
                g = gs[n]
                m = jnp.max(g, axis=0, keepdims=True)
                first = jnp.min(jnp.where(g == m, brow, nbp), axis=0, keepdims=True)
                pick = brow == first
                sels[n] = jnp.logical_or(sels[n], jnp.logical_and(pick, r < i))
                gs[n] = jnp.where(pick, -jnp.inf, g)
        for n, (i, qrows, q) in enumerate(items):
            bias_a = jnp.where(sels[2 * n], 0.0, NEG_BIG)
            bias_b = jnp.where(sels[2 * n + 1], 0.0, NEG_BIG)
            bias = jnp.concatenate([bias_b, gap, bias_a, gap], axis=0).T
            qs_ = q * scale
            qa_s[0, qrows, :] = jnp.where(head_a, qs_, bias).astype(BF16)
            qa_s[1, qrows, :] = jnp.where(head_a, bias, qs_).astype(BF16)
        return carry

    lax.fori_loop(0, nb // gq, gating, 0)

    sb = 2 * blk
    causal = (lax.broadcasted_iota(jnp.int32, (sb, sb), 1)
              <= lax.broadcasted_iota(jnp.int32, (sb, sb), 0))
    head_a2 = lax.broadcasted_iota(jnp.int32, (sb, LANES), 1) < MOBA_DH
    ntile = sb // LANES

    def rows_of(idx):
        return pl.ds(pl.multiple_of(idx * sb, sb), sb)

    def row_max(s):
        parts = [s[:, t * LANES:(t + 1) * LANES] for t in range(ntile)]
        return jnp.maximum(jnp.maximum(parts[0], parts[1]), jnp.maximum(parts[2], parts[3]))

    def loop_by_two(n, steps):
        def body(t, c):
            steps(2 * t, 2)
            return c
        lax.fori_loop(0, n // 2, body, 0)

        @pl.when(n % 2 == 1)
        def _():
            steps(n - 1, 1)

    def qpair(qi, carry):
        qrows = rows_of(qi)

        for h in range(2):
            m_run[h] = jnp.full((sb, LANES), NEG_BIG, F32)

        def pass1(j0, count):
            dots = [(h, j0 + d, _dot_nt(qa_s[h, qrows, :], ka_s[h, rows_of(j0 + d), :]))
                    for d in range(count) for h in range(2)]
            for h, j, s in dots:
                s = s * LOG2E
                s_all[h, j] = s
                m_run[h] = jnp.maximum(m_run[h], row_max(s))

        loop_by_two(qi, pass1)
        for h in range(2):
            s = jnp.where(causal, _dot_nt(qa_s[h, qrows, :], ka_s[h, qrows, :]) * LOG2E, NEG_BIG)
            s_all[h, qi] = s
            m_run[h] = jnp.maximum(m_run[h], row_max(s))

        for h in range(2):
            m_run[h] = jnp.broadcast_to(jnp.max(m_run[h], axis=-1, keepdims=True), (sb, LANES))
            acc_s[h] = jnp.zeros((sb, LANES), F32)

        def pass2(j0, count):
            krows = pl.ds(pl.multiple_of(j0 * sb, sb), count * sb)
            for h in range(2):
                mb = m_run[h]
                ps = [jnp.exp2(s_all[h, j0 + d, :, t * LANES:(t + 1) * LANES] - mb)
                      for d in range(count) for t in range(ntile)]
                acc_s[h] += _dot(jnp.concatenate(ps, axis=1).astype(BF16), vb_s[h, krows, :])

        loop_by_two(qi + 1, pass2)
        acc_a, acc_b = acc_s[0], acc_s[1]
        o_a = acc_a / jnp.max(jnp.where(head_a2, -jnp.inf, acc_a), axis=-1, keepdims=True)
        o_b = acc_b / jnp.max(jnp.where(head_a2, acc_b, -jnp.inf), axis=-1, keepdims=True)
        o_ref[qrows, :] = jnp.where(head_a2, o_a, o_b).astype(o_ref.dtype)
        return carry

    lax.fori_loop(0, nb // 2, qpair, 0)


def _moba(qkvb, cos_t, sin_t, batch, seq):
    t = batch * seq
    nb = seq // MOBA_BLOCK
    assert nb % 2 == 0, "query and key blocks are processed in pairs"
    sb = 2 * MOBA_BLOCK
    npair = MOBA_W // LANES
    col = lambda off: pl.BlockSpec((seq, LANES), lambda b, p: (b, off + p))
    return pl.pallas_call(
        functools.partial(_moba_kernel, nb=nb),
        grid=(batch, npair),
        in_specs=[col(0), col(npair), col(2 * npair), _const_spec(cos_t.shape), _const_spec(sin_t.shape)],
        out_specs=col(0),
        out_shape=jax.ShapeDtypeStruct((t, MOBA_W), BF16),
        scratch_shapes=[
            pltpu.VMEM((2, seq, LANES), BF16),
            pltpu.VMEM((2, seq, LANES), BF16),
            pltpu.VMEM((LANES, LANES), F32),
            pltpu.VMEM((2, seq, LANES), BF16),
            pltpu.VMEM((2, nb // 2, sb, sb), F32),
            pltpu.VMEM((2, sb, LANES), F32),
            pltpu.VMEM((2, sb, LANES), F32),
        ],
        compiler_params=pltpu.CompilerParams(dimension_semantics=("arbitrary", "arbitrary"),
                                             vmem_limit_bytes=VMEM_LIMIT),
        name="moba",
    )(qkvb, qkvb, qkvb, cos_t, sin_t)


def _mixffn_kernel(oa_ref, ob_ref, x_ref, wo_ref, g1_ref, b1_ref, wup_ref, cw_ref, cb_ref, wdn_ref,
                   g2_ref, b2_ref, o_ref, buf_g, buf_v, carry_g, carry_v, hid_s, *, tm, seq, tf):
    nf = D_FF // tf
    hist = SUBLANES
    seq_start = (pl.program_id(0) * tm) % seq == 0
    mix = _dot(oa_ref[...], wo_ref[0:GDN_W, :]) + _dot(ob_ref[...], wo_ref[GDN_W:, :])
    x1 = _layer_norm(DEEPNORM_ALPHA * x_ref[...] + mix, g1_ref[...], b1_ref[...])
    xb = x1.astype(BF16)
    for f in range(nf):
        halves = []
        for buf, carry, off in ((buf_g, carry_g, 0), (buf_v, carry_v, D_FF)):
            cols = slice(off + f * tf, off + (f + 1) * tf)
            up = _dot(xb, wup_ref[:, cols])
            buf[0:hist, :] = jnp.where(seq_start, 0.0, carry[f])
            buf[hist:hist + tm, :] = up
            carry[f] = up[tm - hist:tm, :]
            y = up * cw_ref[FFN_CONV - 1:FFN_CONV, cols]
            for j in range(FFN_CONV - 1):
                r0 = hist - (FFN_CONV - 1) + j
                y = y + buf[r0:r0 + tm, :] * cw_ref[j:j + 1, cols]
            halves.append(y + cb_ref[:, cols])
        hid_s[:, f * tf:(f + 1) * tf] = (_silu(halves[0]) * halves[1]).astype(BF16)
    y = DEEPNORM_ALPHA * x1 + _dot(hid_s[...], wdn_ref[...])
    o_ref[...] = _layer_norm(y, g2_ref[...], b2_ref[...])


def _mixffn(o_a, o_b, x2, w_out, g1, b1, w_up, conv_w, conv_b, w_down, g2, b2, seq, tm, tf):
    t = x2.shape[0]
    nf = D_FF // tf
    row = lambda w: pl.BlockSpec((tm, w), lambda i: (i, 0))
    consts = (w_out, g1, b1, w_up, conv_w, conv_b, w_down, g2, b2)
    return pl.pallas_call(
        functools.partial(_mixffn_kernel, tm=tm, seq=seq, tf=tf),
        grid=(t // tm,),
        in_specs=[row(GDN_W), row(MOBA_W), row(D_MODEL)] + [_const_spec(c.shape) for c in consts],
        out_specs=row(D_MODEL),
        out_shape=jax.ShapeDtypeStruct((t, D_MODEL), F32),
        scratch_shapes=[
            pltpu.VMEM((tm + SUBLANES, tf), F32), pltpu.VMEM((tm + SUBLANES, tf), F32),
            pltpu.VMEM((nf, SUBLANES, tf), F32), pltpu.VMEM((nf, SUBLANES, tf), F32),
            pltpu.VMEM((tm, D_FF), BF16),
        ],
        compiler_params=pltpu.CompilerParams(dimension_semantics=("arbitrary",),
                                             vmem_limit_bytes=VMEM_LIMIT),
        name="mixffn",
    )(o_a, o_b, x2, *consts)


def _rope_tables(seq):
    half = ROPE_DIMS // 2
    inv = ROPE_THETA ** (-jnp.arange(half, dtype=F32) / half)
    ang = jnp.arange(seq, dtype=jnp.int32).astype(F32)[:, None] * inv[None, :]
    cos, sin = jnp.cos(ang), jnp.sin(ang)
    rest = MOBA_DH - ROPE_DIMS
    cos_h = jnp.concatenate([cos, cos, jnp.ones((seq, rest), F32)], -1)
    sin_h = jnp.concatenate([-sin, sin, jnp.zeros((seq, rest), F32)], -1)
    reps = LANES // MOBA_DH
    return jnp.tile(cos_h, (1, reps)), jnp.tile(sin_h, (1, reps))


def _head_rows(v):
    col = jnp.pad(v.astype(F32), (0, SUBLANES - v.shape[0]))
    return jnp.broadcast_to(col[:, None], (SUBLANES, LANES))


def kernel(x, w_in, gdn_conv_w, gdn_a_log, gdn_dt_bias, gdn_norm_g, w_out, ln1_g, ln1_b,
           w_up, ffn_conv_w, ffn_conv_b, w_down, ln2_g, ln2_b):
    batch, seq, d = x.shape
    t = batch * seq
    tm = min(512, seq)
    tl = min(1024, seq)
    tf = 256
    cos_t, sin_t = _rope_tables(seq)
    g_off = 3 * GDN_W
    z_off = g_off + 2 * GDN_HEADS
    x2 = x.reshape(t, d)
    for l in range(DEPTH):
        w = w_in[l]
        w_main = jnp.concatenate([w[:, :g_off], w[:, z_off:]], axis=1).astype(BF16)
        w_gate = jnp.pad(w[:, g_off:z_off], ((0, 0), (0, LANES - 2 * GDN_HEADS))).astype(BF16)
        qkva, z, gates, qkvb = _inproj(x2, w_main, w_gate, tm)
        o_a = _gdn(qkva, z, gates, gdn_conv_w[l], _head_rows(gdn_a_log[l]), _head_rows(gdn_dt_bias[l]),
                   gdn_norm_g[l][None, :], batch, seq, tl)
        o_b = _moba(qkvb, cos_t, sin_t, batch, seq)
        x2 = _mixffn(o_a, o_b, x2, w_out[l].astype(BF16), ln1_g[l][None, :], ln1_b[l][None, :],
                     w_up[l].astype(BF16), ffn_conv_w[l], ffn_conv_b[l][None, :], w_down[l].astype(BF16),
                     ln2_g[l][None, :], ln2_b[l][None, :], seq, tm, tf)
    return x2.reshape(batch, seq, d)
```
